```python
import jax, jax.numpy as jnp
from jax import lax
import numpy as np

D_MODEL = 2048
BATCH = 2
SEQ = 16384
DEPTH = 2
DEC_BATCH = 4
DEC_SEQ = 2048
PAST_LEN = 128

N_MIXERS = 2
N_META = 16
GRID_W = 64
HEAD_DIM = 128
NA_HEADS = D_MODEL // HEAD_DIM
NA_WIN_ROWS_MAX = 8
NA_WIN_COLS = 16
GQA_Q_HEADS = D_MODEL // HEAD_DIM
GQA_KV_HEADS = 4
GQA_GROUP = GQA_Q_HEADS // GQA_KV_HEADS
ROPE_THETA = 10000.0
Q_BLOCK = 128
D_FF = 5632
CONV_W = 3
EPS = 1e-6
N_A = (DEPTH + 1) // 2
N_B = DEPTH // 2

kernel_name = "hybrid_natten_gqa_convglu_encoder"


def rms_norm(x, g):
    xf = x.astype(jnp.float32)
    y = xf * lax.rsqrt(jnp.mean(xf * xf, axis=-1, keepdims=True) + EPS)
    return (y * g.astype(jnp.float32)).astype(x.dtype)


def neighbourhood_attention(h, w_qkv, rpb, w_o):
    B, L, _ = h.shape
    T = L - N_META
    rows = T // GRID_W
    wr = min(NA_WIN_ROWS_MAX, rows)
    scale = HEAD_DIM ** -0.5
    qkv = jnp.einsum('bld,de->ble', h, w_qkv).reshape(B, L, 3, NA_HEADS, HEAD_DIM)
    q = qkv[:, :, 0] * jnp.asarray(scale, h.dtype)
    k = qkv[:, :, 1]
    v = qkv[:, :, 2]
    qm, km, vm = q[:, :N_META], k[:, :N_META], v[:, :N_META]
    qg = q[:, N_META:].reshape(B, rows, GRID_W, NA_HEADS, HEAD_DIM)
    kg = k[:, N_META:].reshape(B, rows, GRID_W, NA_HEADS, HEAD_DIM)
    vg = v[:, N_META:].reshape(B, rows, GRID_W, NA_HEADS, HEAD_DIM)

    s_mm = jnp.einsum('bqhd,bkhd->bhqk', qm, km).astype(jnp.float32)
    p_mm = jax.nn.softmax(s_mm, axis=-1).astype(v.dtype)
    o_meta = jnp.einsum('bhqk,bkhd->bqhd', p_mm, vm).reshape(B, N_META, D_MODEL)

    cols = jnp.arange(GRID_W)
    col_start = jnp.clip(cols - NA_WIN_COLS // 2, 0, GRID_W - NA_WIN_COLS)
    col_idx = col_start[:, None] + jnp.arange(NA_WIN_COLS)[None, :]
    col_bias_idx = col_idx - cols[:, None] + (NA_WIN_COLS - 1)

    def row_block(r):
        r0 = jnp.clip(r - wr // 2, 0, rows - wr)
        q_r = lax.dynamic_index_in_dim(qg, r, axis=1, keepdims=False)
        k_rows = lax.dynamic_slice_in_dim(kg, r0, wr, axis=1)
        v_rows = lax.dynamic_slice_in_dim(vg, r0, wr, axis=1)
        k_win = k_rows[:, :, col_idx]
        v_win = v_rows[:, :, col_idx]
        row_bias_idx = r0 + jnp.arange(wr) - r + (NA_WIN_ROWS_MAX - 1)
        bias = rpb[:, row_bias_idx[None, :, None], col_bias_idx[:, None, :]]
        s_loc = jnp.einsum('bqhd,brqchd->bhqrc', q_r, k_win).astype(jnp.float32)
        s_loc = s_loc + bias.astype(jnp.float32)[None]
        s_met = jnp.einsum('bqhd,bkhd->bhqk', q_r, km).astype(jnp.float32)
        s = jnp.concatenate([s_met, s_loc.reshape(B, NA_HEADS, GRID_W, wr * NA_WIN_COLS)], axis=-1)
        p = jax.nn.softmax(s, axis=-1).astype(v.dtype)
        p_met = p[..., :N_META]
        p_loc = p[..., N_META:].reshape(B, NA_HEADS, GRID_W, wr, NA_WIN_COLS)
        return (jnp.einsum('bhqk,bkhd->bqhd', p_met, vm)
                + jnp.einsum('bhqrc,brqchd->bqhd', p_loc, v_win))

    o_grid = lax.map(row_block, jnp.arange(rows))
    o_grid = jnp.transpose(o_grid, (1, 0, 2, 3, 4)).reshape(B, T, D_MODEL)
    o = jnp.concatenate([o_meta, o_grid], axis=1)
    return jnp.einsum('bld,de->ble', o, w_o)


def axial_rope_tables(T, dtype):
    t = jnp.arange(T)
    row = (t // GRID_W).astype(jnp.float32)
    col = (t % GRID_W).astype(jnp.float32)
    n_axis_pairs = HEAD_DIM // 4
    inv_freq = ROPE_THETA ** (-jnp.arange(n_axis_pairs, dtype=jnp.float32) / n_axis_pairs)
    ang = jnp.concatenate([row[:, None] * inv_freq[None], col[:, None] * inv_freq[None]], axis=-1)
    ang = jnp.concatenate([jnp.zeros((N_META, HEAD_DIM // 2), jnp.float32), ang], axis=0)
    return jnp.cos(ang).astype(dtype), jnp.sin(ang).astype(dtype)


def apply_rope(x, cos, sin):
    half = HEAD_DIM // 2
    x1, x2 = x[..., :half], x[..., half:]
    c = cos[None, :, None, :]
    s = sin[None, :, None, :]
    return jnp.concatenate([x1 * c - x2 * s, x1 * s + x2 * c], axis=-1)


def gqa_attention(h, w_qkv, q_norm, k_norm, w_o):
    B, L, _ = h.shape
    T = L - N_META
    scale = HEAD_DIM ** -0.5
    qkv = jnp.einsum('bld,de->ble', h, w_qkv)
    nq = GQA_Q_HEADS * HEAD_DIM
    nk = GQA_KV_HEADS * HEAD_DIM
    q = qkv[..., :nq].reshape(B, L, GQA_Q_HEADS, HEAD_DIM)
    k = qkv[..., nq:nq + nk].reshape(B, L, GQA_KV_HEADS, HEAD_DIM)
    v = qkv[..., nq + nk:].reshape(B, L, GQA_KV_HEADS, HEAD_DIM)
    q = rms_norm(q, q_norm)
    k = rms_norm(k, k_norm)
    cos, sin = axial_rope_tables(T, h.dtype)
    q = apply_rope(q, cos, sin) * jnp.asarray(scale, h.dtype)
    k = apply_rope(k, cos, sin)
    q = q.reshape(B, L, GQA_KV_HEADS, GQA_GROUP, HEAD_DIM)

    def attend(qb):
        s = jnp.einsum('bqkgd,bskd->bkgqs', qb, k).astype(jnp.float32)
        p = jax.nn.softmax(s, axis=-1).astype(v.dtype)
        return jnp.einsum('bkgqs,bskd->bqkgd', p, v)

    o_meta = attend(q[:, :N_META]).reshape(B, N_META, D_MODEL)
    nb = T // Q_BLOCK
    qb = q[:, N_META:].reshape(B, nb, Q_BLOCK, GQA_KV_HEADS, GQA_GROUP, HEAD_DIM)
    qb = jnp.transpose(qb, (1, 0, 2, 3, 4, 5))
    o_blocks = lax.map(attend, qb)
    o_grid = jnp.transpose(o_blocks, (1, 0, 2, 3, 4, 5)).reshape(B, T, D_MODEL)
    o = jnp.concatenate([o_meta, o_grid], axis=1)
    return jnp.einsum('bld,de->ble', o, w_o)


def conv_glu(h, w_in, conv_w, conv_b, w_out):
    L = h.shape[1]
    u = jnp.einsum('bld,df->blf', h, w_in)
    g, val = u[..., :D_FF], u[..., D_FF:]
    pad = CONV_W // 2
    gp = jnp.pad(g, ((0, 0), (pad, CONV_W - 1 - pad), (0, 0)))
    gc = conv_b[None, None, :]
    for tap in range(CONV_W):
        gc = gc + gp[:, tap:tap + L] * conv_w[tap][None, None, :]
    a = jax.nn.silu(gc) * val
    return jnp.einsum('blf,fd->bld', a, w_out)


def trunk(x, meta_tokens, a_norm, a_w_qkv, a_rpb, a_w_o, b_norm, b_w_qkv, b_q_norm, b_k_norm, b_w_o,
          ffn_norm, ffn_w_in, ffn_conv_w, ffn_conv_b, ffn_w_out, final_norm):
    B = x.shape[0]
    meta = jnp.broadcast_to(meta_tokens.astype(x.dtype)[None], (B, N_META, D_MODEL))
    h = jnp.concatenate([meta, x], axis=1)
    for i in range(DEPTH):
        j = i // N_MIXERS
        if i % N_MIXERS == 0:
            h = h + neighbourhood_attention(rms_norm(h, a_norm[j]), a_w_qkv[j], a_rpb[j], a_w_o[j])
        else:
            h = h + gqa_attention(rms_norm(h, b_norm[j]), b_w_qkv[j], b_q_norm[j], b_k_norm[j], b_w_o[j])
        h = h + conv_glu(rms_norm(h, ffn_norm[i]), ffn_w_in[i], ffn_conv_w[i], ffn_conv_b[i], ffn_w_out[i])
    h = rms_norm(h, final_norm)
    return h[:, N_META:]


def setup_inputs(seed: int = 0) -> dict:
    key = jax.random.key(seed)
    ks = jax.random.split(key, 20)
    f32 = jnp.float32
    d_qkv_b = (GQA_Q_HEADS + 2 * GQA_KV_HEADS) * HEAD_DIM
    nrm = lambda k, shape, s: jax.random.normal(k, shape, f32) * s
    gain = lambda k, shape: 1.0 + 0.01 * jax.random.normal(k, shape, f32)
    return {
        "x_prompt": nrm(ks[0], (BATCH, SEQ, D_MODEL), 1.0),
        "x_sample": nrm(ks[1], (DEC_BATCH, DEC_SEQ, D_MODEL), 1.0),
        "meta_tokens": nrm(ks[2], (N_META, D_MODEL), 1.0),
        "a_norm": gain(ks[3], (N_A, D_MODEL)),
        "a_w_qkv": nrm(ks[4], (N_A, D_MODEL, 3 * D_MODEL), D_MODEL ** -0.5),
        "a_rpb": nrm(ks[5], (N_A, NA_HEADS, 2 * NA_WIN_ROWS_MAX - 1, 2 * NA_WIN_COLS - 1), 0.1),
        "a_w_o": nrm(ks[6], (N_A, D_MODEL, D_MODEL), D_MODEL ** -0.5),
        "b_norm": gain(ks[7], (N_B, D_MODEL)),
        "b_w_qkv": nrm(ks[8], (N_B, D_MODEL, d_qkv_b), D_MODEL ** -0.5),
        "b_q_norm": gain(ks[9], (N_B, HEAD_DIM)),
        "b_k_norm": gain(ks[10], (N_B, HEAD_DIM)),
        "b_w_o": nrm(ks[11], (N_B, D_MODEL, D_MODEL), D_MODEL ** -0.5),
        "ffn_norm": gain(ks[12], (DEPTH, D_MODEL)),
        "ffn_w_in": nrm(ks[13], (DEPTH, D_MODEL, 2 * D_FF), D_MODEL ** -0.5),
        "ffn_conv_w": nrm(ks[14], (DEPTH, CONV_W, D_FF), CONV_W ** -0.5),
        "ffn_conv_b": nrm(ks[15], (DEPTH, D_FF), 0.01),
        "ffn_w_out": nrm(ks[16], (DEPTH, D_FF, D_MODEL), D_FF ** -0.5),
        "final_norm": gain(ks[17], (D_MODEL,)),
    }


def reference(x_prompt, x_sample, meta_tokens, a_norm, a_w_qkv, a_rpb, a_w_o, b_norm, b_w_qkv, b_q_norm,
              b_k_norm, b_w_o, ffn_norm, ffn_w_in, ffn_conv_w, ffn_conv_b, ffn_w_out, final_norm):
    y_prompt = trunk(x_prompt, meta_tokens, a_norm, a_w_qkv, a_rpb, a_w_o, b_norm, b_w_qkv, b_q_norm,
                     b_k_norm, b_w_o, ffn_norm, ffn_w_in, ffn_conv_w, ffn_conv_b, ffn_w_out, final_norm)
    y_sample = trunk(x_sample, meta_tokens, a_norm, a_w_qkv, a_rpb, a_w_o, b_norm, b_w_qkv, b_q_norm,
                     b_k_norm, b_w_o, ffn_norm, ffn_w_in, ffn_conv_w, ffn_conv_b, ffn_w_out, final_norm)
    return (y_prompt, y_sample)
```

```python
import functools

import numpy as np
import jax
import jax.numpy as jnp
from jax import lax
from jax.experimental import pallas as pl
from jax.experimental.pallas import tpu as pltpu

HEAD_DIM = 128
N_META = 16
GRID_W = 64
NA_WIN_ROWS = 8
NA_WIN_COLS = 16
ROPE_THETA = 10000.0
EPS = 1e-6
CONV_W = 3

META_BLOCK = 128
META_OFF = META_BLOCK - N_META
NA_QROWS = 4
NA_KROWS = 12
HALO = 16
GQA_TK = 512
NEG = -1e30
COMPUTE_DTYPE = jnp.bfloat16
V7X_VMEM_LIMIT_BYTES = 60 * 1024 * 1024

F32 = jnp.float32


def _dot(a, b):
    return jnp.dot(a, b, preferred_element_type=F32)


def _dot_nt(a, b):
    return lax.dot_general(a, b, (((1,), (1,)), ((), ())), preferred_element_type=F32)


def _rms(x, gain):
    ms = jnp.mean(x * x, axis=-1, keepdims=True)
    return x * lax.rsqrt(ms + EPS) * gain


def _params(semantics):
    return pltpu.CompilerParams(dimension_semantics=semantics, vmem_limit_bytes=V7X_VMEM_LIMIT_BYTES)


def _row_tile(m):
    for t in (768, 512, 384, 256, 128):
        if m % t == 0:
            return t
    raise ValueError(f"row count {m} is not a multiple of 128")


def _col_tile(n):
    for t in (512, 256, 128):
        if n % t == 0:
            return t
    raise ValueError(f"column count {n} is not a multiple of 128")


def _norm_proj_kernel(*refs, n_q_tiles, n_k_tiles, heads_per_tile, qk_norm, scale):
    if qk_norm:
        x_ref, g_ref, w_ref, qg_ref, kg_ref, cos_ref, sin_ref, o_ref, xn_ref = refs
    else:
        x_ref, g_ref, w_ref, o_ref, xn_ref = refs
    j = pl.program_id(1)

    @pl.when(j == 0)
    def _():
        xn_ref[...] = _rms(x_ref[...], g_ref[...]).astype(xn_ref.dtype)

    acc = _dot(xn_ref[...], w_ref[...])

    def store(fn):
        for hh in range(heads_per_tile):
            o_ref[hh] = fn(acc[:, hh * HEAD_DIM:(hh + 1) * HEAD_DIM]).astype(o_ref.dtype)

    def rotary(y, gain_ref):
        yn = _rms(y, gain_ref[...])
        return yn * cos_ref[...] + pltpu.roll(yn, HEAD_DIM // 2, axis=1) * sin_ref[...]

    @pl.when(j < n_q_tiles)
    def _():
        if qk_norm:
            store(lambda y: rotary(y, qg_ref) * scale)
        else:
            store(lambda y: y * scale)

    @pl.when(jnp.logical_and(j >= n_q_tiles, j < n_q_tiles + n_k_tiles))
    def _():
        if qk_norm:
            store(lambda y: rotary(y, kg_ref))
        else:
            store(lambda y: y)

    @pl.when(j >= n_q_tiles + n_k_tiles)
    def _():
        store(lambda y: y)


def _norm_proj(h, gain, w, n_q_heads, n_k_heads, scale, qk=None):
    m, d = h.shape
    n = w.shape[1]
    tm = _row_tile(m)
    tn = 4 * HEAD_DIM if (n_k_heads * HEAD_DIM) % (4 * HEAD_DIM) == 0 else HEAD_DIM
    hpt = tn // HEAD_DIM
    assert n % tn == 0 and (n_q_heads * HEAD_DIM) % tn == 0 and (n_k_heads * HEAD_DIM) % tn == 0
    kern = functools.partial(
        _norm_proj_kernel, n_q_tiles=n_q_heads // hpt, n_k_tiles=n_k_heads // hpt,
        heads_per_tile=hpt, qk_norm=qk is not None, scale=scale)
    in_specs = [
        pl.BlockSpec((tm, d), lambda i, j: (i, 0)),
        pl.BlockSpec((1, d), lambda i, j: (0, 0)),
        pl.BlockSpec((d, tn), lambda i, j: (0, j)),
    ]
    args = [h, gain.reshape(1, d), w]
    if qk is not None:
        q_gain, k_gain, cos, sin = qk
        in_specs += [
            pl.BlockSpec((1, HEAD_DIM), lambda i, j: (0, 0)),
            pl.BlockSpec((1, HEAD_DIM), lambda i, j: (0, 0)),
            pl.BlockSpec((tm, HEAD_DIM), lambda i, j: (i, 0)),
            pl.BlockSpec((tm, HEAD_DIM), lambda i, j: (i, 0)),
        ]
        args += [q_gain.reshape(1, HEAD_DIM), k_gain.reshape(1, HEAD_DIM), cos, sin]
    return pl.pallas_call(
        kern,
        grid=(m // tm, n // tn),
        in_specs=in_specs,
        out_specs=pl.BlockSpec((hpt, tm, HEAD_DIM), lambda i, j: (j, i, 0)),
        out_shape=jax.ShapeDtypeStruct((n // HEAD_DIM, m, HEAD_DIM), COMPUTE_DTYPE),
        scratch_shapes=[pltpu.VMEM((tm, d), COMPUTE_DTYPE)],
        compiler_params=_params(("arbitrary", "arbitrary")),
        name="norm_proj_rope" if qk is not None else "norm_proj",
    )(*args)


def _meta_key_bias():
    col = lax.broadcasted_iota(jnp.int32, (1, META_BLOCK), 1)
    return jnp.where(col >= META_OFF, 0.0, NEG).astype(F32)


def _na_kernel(q_ref, k_ref, v_ref, bias_ref, o_ref, *, rows):
    n_blocks = rows // NA_QROWS
    tq = NA_QROWS * GRID_W
    tk = NA_KROWS * GRID_W
    meta_bias = _meta_key_bias()
    k0 = k_ref[0:META_BLOCK, :]
    v0 = v_ref[0:META_BLOCK, :]

    s0 = _dot_nt(q_ref[0:META_BLOCK, :], k0) + meta_bias
    p0 = jnp.exp(s0 - jnp.max(s0, axis=-1, keepdims=True))
    o0 = _dot(p0.astype(v0.dtype), v0) / jnp.sum(p0, axis=-1, keepdims=True)
    row = lax.broadcasted_iota(jnp.int32, (META_BLOCK, 1), 0)
    o_ref[0:META_BLOCK, :] = jnp.where(row >= META_OFF, o0, 0.0).astype(o_ref.dtype)

    def body(i, carry):
        r = i * NA_QROWS
        w0 = jnp.clip(r - NA_WIN_ROWS // 2, 0, rows - NA_KROWS)
        variant = jnp.where(i == 0, 0, jnp.where(i == n_blocks - 1, 2, 1))
        qs = pl.multiple_of(META_BLOCK + r * GRID_W, META_BLOCK)
        ks = pl.multiple_of(META_BLOCK + w0 * GRID_W, META_BLOCK)
        q = q_ref[pl.ds(qs, tq), :]
        kw = k_ref[pl.ds(ks, tk), :]
        vw = v_ref[pl.ds(ks, tk), :]
        s_loc = _dot_nt(q, kw) + bias_ref[variant]
        s_met = _dot_nt(q, k0) + meta_bias
        m = jnp.maximum(jnp.max(s_loc, axis=-1, keepdims=True), jnp.max(s_met, axis=-1, keepdims=True))
        p_loc = jnp.exp(s_loc - m)
        p_met = jnp.exp(s_met - m)
        denom = jnp.sum(p_loc, axis=-1, keepdims=True) + jnp.sum(p_met, axis=-1, keepdims=True)
        o = (_dot(p_met.astype(v0.dtype), v0) + _dot(p_loc.astype(vw.dtype), vw)) / denom
        o_ref[pl.ds(qs, tq), :] = o.astype(o_ref.dtype)
        return carry

    lax.fori_loop(0, n_blocks, body, 0)


def _na_bias_table(rpb):
    qi = np.arange(NA_QROWS)[:, None]
    kj = np.arange(NA_KROWS)[None, :]
    half = NA_WIN_ROWS // 2
    row_lo = [np.zeros_like(qi), qi, np.full_like(qi, half)]
    row_off = [0, -half, -NA_WIN_ROWS]
    ridx, rvalid = [], []
    for lo, off in zip(row_lo, row_off):
        rvalid.append((kj >= lo) & (kj < lo + NA_WIN_ROWS))
        ridx.append(kj + off - qi + (NA_WIN_ROWS - 1))
    ridx = np.stack(ridx)
    rvalid = np.stack(rvalid)
    cols = np.arange(GRID_W)
    cstart = np.clip(cols - NA_WIN_COLS // 2, 0, GRID_W - NA_WIN_COLS)[:, None]
    kc = cols[None, :]
    cvalid = (kc >= cstart) & (kc < cstart + NA_WIN_COLS)
    cidx = kc - cols[:, None] + (NA_WIN_COLS - 1)
    valid = rvalid[:, :, None, :, None] & cvalid[None, None, :, None, :]
    ridx = np.where(rvalid, ridx, 0)[:, :, None, :, None]
    cidx = np.where(cvalid, cidx, 0)[None, None, :, None, :]
    bias = rpb.astype(F32)[:, ridx, cidx]
    bias = jnp.where(valid[None], bias, NEG)
    return bias.reshape(rpb.shape[0], 3, NA_QROWS * GRID_W, NA_KROWS * GRID_W)


def _na_attention(qkv, bias, batch, seq_rows, n_heads):
    _, m, _ = qkv.shape
    lseg = m // batch
    rows = seq_rows
    assert rows % NA_QROWS == 0 and rows >= NA_KROWS
    blk = (None, lseg, HEAD_DIM)
    return pl.pallas_call(
        functools.partial(_na_kernel, rows=rows),
        grid=(batch, n_heads),
        in_specs=[
            pl.BlockSpec(blk, lambda b, h: (h, b, 0)),
            pl.BlockSpec(blk, lambda b, h: (n_heads + h, b, 0)),
            pl.BlockSpec(blk, lambda b, h: (2 * n_heads + h, b, 0)),
            pl.BlockSpec((None,) + bias.shape[1:], lambda b, h: (h, 0, 0, 0)),
        ],
        out_specs=pl.BlockSpec(blk, lambda b, h: (h, b, 0)),
        out_shape=jax.ShapeDtypeStruct((n_heads, m, HEAD_DIM), COMPUTE_DTYPE),
        compiler_params=_params(("arbitrary", "arbitrary")),
        name="na_attention",
    )(qkv, qkv, qkv, bias)


def _gqa_kernel(q_ref, k_ref, v_ref, o_ref, m_ref, l_ref, acc_ref, *, group, tq, n_chunks):
    qb = pl.program_id(2)
    q = jnp.concatenate([q_ref[g] for g in range(group)], axis=0)
    k0 = k_ref[0:META_BLOCK, :]
    v0 = v_ref[0:META_BLOCK, :]
    s = _dot_nt(q, k0) + _meta_key_bias()
    m0 = jnp.max(s, axis=-1, keepdims=True)
    p = jnp.exp(s - m0)
    m_ref[...] = m0
    l_ref[...] = jnp.sum(p, axis=-1, keepdims=True)
    acc_ref[...] = _dot(p.astype(v0.dtype), v0)

    def body(c, carry):
        ks = pl.multiple_of(META_BLOCK + c * GQA_TK, META_BLOCK)
        kc = k_ref[pl.ds(ks, GQA_TK), :]
        vc = v_ref[pl.ds(ks, GQA_TK), :]
        s = _dot_nt(q, kc)
        m_prev = m_ref[...]
        m_new = jnp.maximum(m_prev, jnp.max(s, axis=-1, keepdims=True))
        alpha = jnp.exp(m_prev - m_new)
        p = jnp.exp(s - m_new)
        l_ref[...] = alpha * l_ref[...] + jnp.sum(p, axis=-1, keepdims=True)
        acc_ref[...] = alpha * acc_ref[...] + _dot(p.astype(vc.dtype), vc)
        m_ref[...] = m_new
        return carry

    lax.fori_loop(0, n_chunks, body, 0)
    o = acc_ref[...] / l_ref[...]
    row = qb * tq + lax.broadcasted_iota(jnp.int32, (tq, 1), 0)
    keep = row >= META_OFF
    for g in range(group):
        o_ref[g] = jnp.where(keep, o[g * tq:(g + 1) * tq], 0.0).astype(o_ref.dtype)


def _gqa_attention(qkv, batch, n_q_heads, n_kv_heads):
    _, m, _ = qkv.shape
    lseg = m // batch
    group = n_q_heads // n_kv_heads
    tq = META_BLOCK
    nqb = lseg // tq
    assert (lseg - META_BLOCK) % GQA_TK == 0
    n_chunks = (lseg - META_BLOCK) // GQA_TK
    kv_blk = (None, lseg, HEAD_DIM)
    q_blk = (group, tq, HEAD_DIM)
    mq = group * tq
    return pl.pallas_call(
        functools.partial(_gqa_kernel, group=group, tq=tq, n_chunks=n_chunks),
        grid=(batch, n_kv_heads, nqb),
        in_specs=[
            pl.BlockSpec(q_blk, lambda b, kv, i: (kv, b * nqb + i, 0)),
            pl.BlockSpec(kv_blk, lambda b, kv, i: (n_q_heads + kv, b, 0)),
            pl.BlockSpec(kv_blk, lambda b, kv, i: (n_q_heads + n_kv_heads + kv, b, 0)),
        ],
        out_specs=pl.BlockSpec(q_blk, lambda b, kv, i: (kv, b * nqb + i, 0)),
        out_shape=jax.ShapeDtypeStruct((n_q_heads, m, HEAD_DIM), COMPUTE_DTYPE),
        scratch_shapes=[
            pltpu.VMEM((mq, 1), F32),
            pltpu.VMEM((mq, 1), F32),
            pltpu.VMEM((mq, HEAD_DIM), F32),
        ],
        compiler_params=_params(("arbitrary", "arbitrary", "arbitrary")),
        name="gqa_attention",
    )(qkv, qkv, qkv)


def _out_proj_kernel(o_ref, w_ref, h_ref, out_ref, *, n_heads):
    o = jnp.concatenate([o_ref[hh] for hh in range(n_heads)], axis=1)
    out_ref[...] = h_ref[...] + _dot(o, w_ref[...])


def _out_proj(o, w, h):
    n_heads, m, _ = o.shape
    d = h.shape[1]
    tm = _row_tile(m) // 2 if _row_tile(m) >= 256 else _row_tile(m)
    return pl.pallas_call(
        functools.partial(_out_proj_kernel, n_heads=n_heads),
        grid=(m // tm,),
        in_specs=[
            pl.BlockSpec((n_heads, tm, HEAD_DIM), lambda i: (0, i, 0)),
            pl.BlockSpec(w.shape, lambda i: (0, 0)),
            pl.BlockSpec((tm, d), lambda i: (i, 0)),
        ],
        out_specs=pl.BlockSpec((tm, d), lambda i: (i, 0)),
        out_shape=jax.ShapeDtypeStruct((m, d), F32),
        compiler_params=_params(("arbitrary",)),
        name="out_proj",
    )(o, w, h)


def _ffn_kernel(hp_ref, h_ref, hn_ref, g_ref, wg_ref, wv_ref, cw_ref, cb_ref, wo_ref, out_ref,
                xn_ref, gate_ref, acc_ref, *, tm):
    i = pl.program_id(0)
    f = pl.program_id(1)

    @pl.when(f == 0)
    def _():
        gain = g_ref[...]
        prev = jnp.where(i > 0, _rms(hp_ref[...], gain), 0.0)
        nxt = jnp.where(i < pl.num_programs(0) - 1, _rms(hn_ref[...], gain), 0.0)
        xn_ref[0:HALO, :] = prev.astype(xn_ref.dtype)
        xn_ref[HALO:HALO + tm, :] = _rms(h_ref[...], gain).astype(xn_ref.dtype)
        xn_ref[HALO + tm:, :] = nxt.astype(xn_ref.dtype)
        acc_ref[...] = jnp.zeros_like(acc_ref)

    gate_ref[...] = _dot(xn_ref[...], wg_ref[...])
    val = _dot(xn_ref[HALO:HALO + tm, :], wv_ref[...])
    gc = cb_ref[...]
    for tap in range(CONV_W):
        lo = HALO - CONV_W // 2 + tap
        gc = gc + gate_ref[lo:lo + tm, :] * cw_ref[tap:tap + 1, :]
    a = gc * (1.0 / (1.0 + jnp.exp(-gc))) * val
    acc_ref[...] += _dot(a.astype(wo_ref.dtype), wo_ref[...])

    @pl.when(f == pl.num_programs(1) - 1)
    def _():
        out_ref[...] = h_ref[...] + acc_ref[...]


def _ffn(h, gain, w_in, conv_w, conv_b, w_out):
    m, d = h.shape
    d_ff = w_out.shape[0]
    tm = _row_tile(m)
    tf = _col_tile(d_ff)
    n_i = m // tm
    hb = tm // HALO
    return pl.pallas_call(
        functools.partial(_ffn_kernel, tm=tm),
        grid=(n_i, d_ff // tf),
        in_specs=[
            pl.BlockSpec((HALO, d), lambda i, f: (jnp.maximum(i * hb - 1, 0), 0)),
            pl.BlockSpec((tm, d), lambda i, f: (i, 0)),
            pl.BlockSpec((HALO, d), lambda i, f: (jnp.minimum((i + 1) * hb, n_i * hb - 1), 0)),
            pl.BlockSpec((1, d), lambda i, f: (0, 0)),
            pl.BlockSpec((d, tf), lambda i, f: (0, f)),
            pl.BlockSpec((d, tf), lambda i, f: (0, d_ff // tf + f)),
            pl.BlockSpec((CONV_W, tf), lambda i, f: (0, f)),
            pl.BlockSpec((1, tf), lambda i, f: (0, f)),
            pl.BlockSpec((tf, d), lambda i, f: (f, 0)),
        ],
        out_specs=pl.BlockSpec((tm, d), lambda i, f: (i, 0)),
        out_shape=jax.ShapeDtypeStruct((m, d), F32),
        scratch_shapes=[
            pltpu.VMEM((tm + 2 * HALO, d), COMPUTE_DTYPE),
            pltpu.VMEM((tm + 2 * HALO, tf), F32),
            pltpu.VMEM((tm, d), F32),
        ],
        compiler_params=_params(("arbitrary", "arbitrary")),
        name="conv_glu",
    )(h, h, h, gain.reshape(1, d), w_in, w_in, conv_w, conv_b.reshape(1, d_ff), w_out)


def _final_norm_kernel(h_ref, g_ref, o_ref):
    o_ref[...] = _rms(h_ref[...], g_ref[...])


def _final_norm(h, gain, batch, t):
    m, d = h.shape
    blocks_per_seq = (m // batch) // META_BLOCK
    tt = META_BLOCK
    return pl.pallas_call(
        _final_norm_kernel,
        grid=(batch, t // tt),
        in_specs=[
            pl.BlockSpec((tt, d), lambda b, i: (b * blocks_per_seq + 1 + i, 0)),
            pl.BlockSpec((1, d), lambda b, i: (0, 0)),
        ],
        out_specs=pl.BlockSpec((None, tt, d), lambda b, i: (b, i, 0)),
        out_shape=jax.ShapeDtypeStruct((batch, t, d), F32),
        compiler_params=_params(("arbitrary", "arbitrary")),
        name="final_norm",
    )(h, gain.reshape(1, d))


def _rope_tables(batch, t):
    tok = jnp.arange(t)
    row = (tok // GRID_W).astype(F32)
    col = (tok % GRID_W).astype(F32)
    n_axis_pairs = HEAD_DIM // 4
    inv_freq = ROPE_THETA ** (-jnp.arange(n_axis_pairs, dtype=F32) / n_axis_pairs)
    ang = jnp.concatenate([row[:, None] * inv_freq[None], col[:, None] * inv_freq[None]], axis=-1)
    ang = jnp.concatenate([jnp.zeros((META_BLOCK, HEAD_DIM // 2), F32), ang], axis=0)
    cos, sin = jnp.cos(ang), jnp.sin(ang)
    cos = jnp.concatenate([cos, cos], axis=-1)
    sin = jnp.concatenate([-sin, sin], axis=-1)
    return jnp.tile(cos, (batch, 1)), jnp.tile(sin, (batch, 1))


def _trunk(x, meta_tokens, w):
    batch, t, d = x.shape
    assert t % GRID_W == 0 and d % HEAD_DIM == 0
    rows = t // GRID_W
    n_heads = d // HEAD_DIM
    scale = HEAD_DIM ** -0.5
    meta = jnp.broadcast_to(meta_tokens.astype(x.dtype)[None], (batch, N_META, d))
    h = jnp.concatenate([jnp.zeros((batch, META_OFF, d), x.dtype), meta, x], axis=1)
    h = h.reshape(batch * (META_BLOCK + t), d)
    depth = w["ffn_norm"].shape[0]
    rope = None
    for i in range(depth):
        j = i // 2
        if i % 2 == 0:
            qkv = _norm_proj(h, w["a_norm"][j], w["a_w_qkv"][j], n_heads, n_heads, scale)
            o = _na_attention(qkv, _na_bias_table(w["a_rpb"][j]), batch, rows, n_heads)
            h = _out_proj(o, w["a_w_o"][j], h)
        else:
            n_kv = (w["b_w_qkv"].shape[-1] // HEAD_DIM - n_heads) // 2
            if rope is None:
                rope = _rope_tables(batch, t)
            qkv = _norm_proj(h, w["b_norm"][j], w["b_w_qkv"][j], n_heads, n_kv, scale,
                             qk=(w["b_q_norm"][j], w["b_k_norm"][j]) + rope)
            o = _gqa_attention(qkv, batch, n_heads, n_kv)
            h = _out_proj(o, w["b_w_o"][j], h)
        h = _ffn(h, w["ffn_norm"][i], w["ffn_w_in"][i], w["ffn_conv_w"][i], w["ffn_conv_b"][i],
                 w["ffn_w_out"][i])
    return _final_norm(h, w["final_norm"], batch, t)


def kernel(x_prompt, x_sample, meta_tokens, a_norm, a_w_qkv, a_rpb, a_w_o, b_norm, b_w_qkv, b_q_norm,
           b_k_norm, b_w_o, ffn_norm, ffn_w_in, ffn_conv_w, ffn_conv_b, ffn_w_out, final_norm):
    cast = lambda a: a.astype(COMPUTE_DTYPE)
    w = dict(a_norm=a_norm, a_w_qkv=cast(a_w_qkv), a_rpb=a_rpb, a_w_o=cast(a_w_o), b_norm=b_norm,
             b_w_qkv=cast(b_w_qkv), b_q_norm=b_q_norm, b_k_norm=b_k_norm, b_w_o=cast(b_w_o),
             ffn_norm=ffn_norm, ffn_w_in=cast(ffn_w_in), ffn_conv_w=ffn_conv_w, ffn_conv_b=ffn_conv_b,
             ffn_w_out=cast(ffn_w_out), final_norm=final_norm)
    return (_trunk(x_prompt, meta_tokens, w), _trunk(x_sample, meta_tokens, w))
```

```python
import functools

import numpy as np
import jax
import jax.numpy as jnp
from jax import lax
from jax.experimental import pallas as pl
from jax.experimental.pallas import tpu as pltpu

HEAD_DIM = 128
N_META = 16
GRID_W = 64
NA_WIN_ROWS = 8
NA_WIN_COLS = 16
ROPE_THETA = 10000.0
EPS = 1e-6
CONV_W = 3

META_BLOCK = 128
META_OFF = META_BLOCK - N_META
NA_QROWS = 4
NA_KROWS = 12
HALO = 16
GQA_TK = 512
NEG = -1e30
LOG2E = 1.4426950408889634
COMPUTE_DTYPE = jnp.bfloat16
V7X_VMEM_LIMIT_BYTES = 60 * 1024 * 1024

F32 = jnp.float32


def _dot(a, b):
    return jnp.dot(a, b, preferred_element_type=F32)


def _dot_nt(a, b):
    return lax.dot_general(a, b, (((1,), (1,)), ((), ())), preferred_element_type=F32)


def _rms(x, gain):
    ms = jnp.mean(x * x, axis=-1, keepdims=True)
    return x * lax.rsqrt(ms + EPS) * gain


def _params(semantics):
    return pltpu.CompilerParams(dimension_semantics=semantics, vmem_limit_bytes=V7X_VMEM_LIMIT_BYTES)


def _row_tile(m):
    for t in (768, 512, 384, 256, 128):
        if m % t == 0:
            return t
    raise ValueError(f"row count {m} is not a multiple of 128")


def _col_tile(n):
    for t in (512, 256, 128):
        if n % t == 0:
            return t
    raise ValueError(f"column count {n} is not a multiple of 128")


def _norm_proj_kernel(*refs, n_q_tiles, n_k_tiles, heads_per_tile, qk_norm, scale):
    if qk_norm:
        x_ref, g_ref, w_ref, qg_ref, kg_ref, cos_ref, sin_ref, o_ref, xn_ref = refs
    else:
        x_ref, g_ref, w_ref, o_ref, xn_ref = refs
    j = pl.program_id(1)

    @pl.when(j == 0)
    def _():
        xn_ref[...] = _rms(x_ref[...], g_ref[...]).astype(xn_ref.dtype)

    acc = _dot(xn_ref[...], w_ref[...])

    def store(fn):
        for hh in range(heads_per_tile):
            o_ref[hh] = fn(acc[:, hh * HEAD_DIM:(hh + 1) * HEAD_DIM]).astype(o_ref.dtype)

    def rotary(y, gain_ref):
        yn = _rms(y, gain_ref[...])
        return yn * cos_ref[...] + pltpu.roll(yn, HEAD_DIM // 2, axis=1) * sin_ref[...]

    @pl.when(j < n_q_tiles)
    def _():
        if qk_norm:
            store(lambda y: rotary(y, qg_ref) * scale)
        else:
            store(lambda y: y * scale)

    @pl.when(jnp.logical_and(j >= n_q_tiles, j < n_q_tiles + n_k_tiles))
    def _():
        if qk_norm:
            store(lambda y: rotary(y, kg_ref))
        else:
            store(lambda y: y)

    @pl.when(j >= n_q_tiles + n_k_tiles)
    def _():
        store(lambda y: y)


def _norm_proj(h, gain, w, n_q_heads, n_k_heads, scale, qk=None):
    m, d = h.shape
    n = w.shape[1]
    tm = _row_tile(m)
    tn = 4 * HEAD_DIM if (n_k_heads * HEAD_DIM) % (4 * HEAD_DIM) == 0 else HEAD_DIM
    hpt = tn // HEAD_DIM
    assert n % tn == 0 and (n_q_heads * HEAD_DIM) % tn == 0 and (n_k_heads * HEAD_DIM) % tn == 0
    kern = functools.partial(
        _norm_proj_kernel, n_q_tiles=n_q_heads // hpt, n_k_tiles=n_k_heads // hpt,
        heads_per_tile=hpt, qk_norm=qk is not None, scale=scale)
    in_specs = [
        pl.BlockSpec((tm, d), lambda i, j: (i, 0)),
        pl.BlockSpec((1, d), lambda i, j: (0, 0)),
        pl.BlockSpec((d, tn), lambda i, j: (0, j)),
    ]
    args = [h, gain.reshape(1, d), w]
    if qk is not None:
        q_gain, k_gain, cos, sin = qk
        in_specs += [
            pl.BlockSpec((1, HEAD_DIM), lambda i, j: (0, 0)),
            pl.BlockSpec((1, HEAD_DIM), lambda i, j: (0, 0)),
            pl.BlockSpec((tm, HEAD_DIM), lambda i, j: (i, 0)),
            pl.BlockSpec((tm, HEAD_DIM), lambda i, j: (i, 0)),
        ]
        args += [q_gain.reshape(1, HEAD_DIM), k_gain.reshape(1, HEAD_DIM), cos, sin]
    return pl.pallas_call(
        kern,
        grid=(m // tm, n // tn),
        in_specs=in_specs,
        out_specs=pl.BlockSpec((hpt, tm, HEAD_DIM), lambda i, j: (j, i, 0)),
        out_shape=jax.ShapeDtypeStruct((n // HEAD_DIM, m, HEAD_DIM), COMPUTE_DTYPE),
        scratch_shapes=[pltpu.VMEM((tm, d), COMPUTE_DTYPE)],
        compiler_params=_params(("arbitrary", "arbitrary")),
        name="norm_proj_rope" if qk is not None else "norm_proj",
    )(*args)


def _meta_key_bias():
    col = lax.broadcasted_iota(jnp.int32, (1, META_BLOCK), 1)
    return jnp.where(col >= META_OFF, 0.0, NEG).astype(F32)


def _na_kernel(q_ref, k_ref, v_ref, bias_ref, o_ref, *, rows):
    n_blocks = rows // NA_QROWS
    tq = NA_QROWS * GRID_W
    tk = NA_KROWS * GRID_W
    meta_bias = _meta_key_bias()
    k0 = k_ref[0:META_BLOCK, :]
    v0 = v_ref[0:META_BLOCK, :]

    s0 = _dot_nt(q_ref[0:META_BLOCK, :], k0) + meta_bias
    p0 = jnp.exp(s0 - jnp.max(s0, axis=-1, keepdims=True))
    o0 = _dot(p0.astype(v0.dtype), v0) / jnp.sum(p0, axis=-1, keepdims=True)
    row = lax.broadcasted_iota(jnp.int32, (META_BLOCK, 1), 0)
    o_ref[0:META_BLOCK, :] = jnp.where(row >= META_OFF, o0, 0.0).astype(o_ref.dtype)

    def body(i, carry):
        r = i * NA_QROWS
        w0 = jnp.clip(r - NA_WIN_ROWS // 2, 0, rows - NA_KROWS)
        variant = jnp.where(i == 0, 0, jnp.where(i == n_blocks - 1, 2, 1))
        qs = pl.multiple_of(META_BLOCK + r * GRID_W, META_BLOCK)
        ks = pl.multiple_of(META_BLOCK + w0 * GRID_W, META_BLOCK)
        q = q_ref[pl.ds(qs, tq), :]
        kw = k_ref[pl.ds(ks, tk), :]
        vw = v_ref[pl.ds(ks, tk), :]
        s_loc = _dot_nt(q, kw) + bias_ref[variant]
        s_met = _dot_nt(q, k0) + meta_bias
        m = jnp.maximum(jnp.max(s_loc, axis=-1, keepdims=True), jnp.max(s_met, axis=-1, keepdims=True))
        p_loc = jnp.exp(s_loc - m)
        p_met = jnp.exp(s_met - m)
        denom = jnp.sum(p_loc, axis=-1, keepdims=True) + jnp.sum(p_met, axis=-1, keepdims=True)
        o = (_dot(p_met.astype(v0.dtype), v0) + _dot(p_loc.astype(vw.dtype), vw)) / denom
        o_ref[pl.ds(qs, tq), :] = o.astype(o_ref.dtype)
        return carry

    lax.fori_loop(0, n_blocks, body, 0)


def _na_bias_table(rpb):
    qi = np.arange(NA_QROWS)[:, None]
    kj = np.arange(NA_KROWS)[None, :]
    half = NA_WIN_ROWS // 2
    row_lo = [np.zeros_like(qi), qi, np.full_like(qi, half)]
    row_off = [0, -half, -NA_WIN_ROWS]
    ridx, rvalid = [], []
    for lo, off in zip(row_lo, row_off):
        rvalid.append((kj >= lo) & (kj < lo + NA_WIN_ROWS))
        ridx.append(kj + off - qi + (NA_WIN_ROWS - 1))
    ridx = np.stack(ridx)
    rvalid = np.stack(rvalid)
    cols = np.arange(GRID_W)
    cstart = np.clip(cols - NA_WIN_COLS // 2, 0, GRID_W - NA_WIN_COLS)[:, None]
    kc = cols[None, :]
    cvalid = (kc >= cstart) & (kc < cstart + NA_WIN_COLS)
    cidx = kc - cols[:, None] + (NA_WIN_COLS - 1)
    valid = rvalid[:, :, None, :, None] & cvalid[None, None, :, None, :]
    n_r, n_c = rpb.shape[1], rpb.shape[2]
    rsel = (np.where(rvalid, ridx, 0)[..., None] == np.arange(n_r)).astype(np.float32)
    csel = (np.arange(n_c)[:, None, None] == np.where(cvalid, cidx, 0)[None]).astype(np.float32)
    hi = lax.Precision.HIGHEST
    by_row = jnp.einsum("hrc,vqjr->hvqjc", rpb.astype(F32), rsel, precision=hi)
    bias = jnp.einsum("hvqjc,cxy->hvqxjy", by_row, csel, precision=hi)
    bias = jnp.where(valid[None], bias, NEG)
    return bias.reshape(rpb.shape[0], 3, NA_QROWS * GRID_W, NA_KROWS * GRID_W)


def _na_attention(qkv, bias, batch, seq_rows, n_heads):
    _, m, _ = qkv.shape
    lseg = m // batch
    rows = seq_rows
    assert rows % NA_QROWS == 0 and rows >= NA_KROWS
    blk = (None, lseg, HEAD_DIM)
    return pl.pallas_call(
        functools.partial(_na_kernel, rows=rows),
        grid=(batch, n_heads),
        in_specs=[
            pl.BlockSpec(blk, lambda b, h: (h, b, 0)),
            pl.BlockSpec(blk, lambda b, h: (n_heads + h, b, 0)),
            pl.BlockSpec(blk, lambda b, h: (2 * n_heads + h, b, 0)),
            pl.BlockSpec((None,) + bias.shape[1:], lambda b, h: (h, 0, 0, 0)),
        ],
        out_specs=pl.BlockSpec(blk, lambda b, h: (h, b, 0)),
        out_shape=jax.ShapeDtypeStruct((n_heads, m, HEAD_DIM), COMPUTE_DTYPE),
        compiler_params=_params(("arbitrary", "arbitrary")),
        name="na_attention",
    )(qkv, qkv, qkv, bias)


def _gqa_kernel(q_ref, k_ref, v_ref, o_ref, vext_ref, s0_ref, s1_ref, p0_ref, p1_ref, a0_ref, a1_ref,
                m_ref, acc_ref, *, group, tq, n_chunks):
    qb = pl.program_id(2)
    n_tiles = GQA_TK // HEAD_DIM
    s_refs, p_refs, a_refs = (s0_ref, s1_ref), (p0_ref, p1_ref), (a0_ref, a1_ref)

    @pl.when(qb == 0)
    def _():
        vext_ref[:, 0:HEAD_DIM] = v_ref[...]
        vext_ref[:, HEAD_DIM:] = jnp.ones((vext_ref.shape[0], HEAD_DIM), vext_ref.dtype)

    def queries():
        return jnp.concatenate([q_ref[g] for g in range(group)], axis=0)

    def chunk_start(c):
        return pl.multiple_of(META_BLOCK + c * GQA_TK, META_BLOCK)

    def scores(c, buf):
        s_refs[buf][...] = _dot_nt(queries(), k_ref[pl.ds(chunk_start(c), GQA_TK), :])

    def softmax(buf):
        for g in range(group):
            r0 = g * tq
            tiles = [s_refs[buf][r0:r0 + tq, t * HEAD_DIM:(t + 1) * HEAD_DIM] for t in range(n_tiles)]
            m_prev = m_ref[r0:r0 + tq, :]
            m_cur = functools.reduce(jnp.maximum, tiles)
            m_new = jnp.maximum(m_prev, jnp.max(m_cur, axis=-1, keepdims=True))
            m_ref[r0:r0 + tq, :] = m_new
            a_refs[buf][r0:r0 + tq, :] = jnp.exp2(m_prev - m_new)
            for t in range(n_tiles):
                p_refs[buf][r0:r0 + tq, t * HEAD_DIM:(t + 1) * HEAD_DIM] = (
                    jnp.exp2(tiles[t] - m_new).astype(p_refs[buf].dtype))

    def accumulate(c, buf):
        pv = _dot(p_refs[buf][...], vext_ref[pl.ds(chunk_start(c), GQA_TK), :])
        alpha = a_refs[buf][...]
        acc_ref[...] = acc_ref[...] * jnp.concatenate([alpha, alpha], axis=1) + pv

    s_meta = _dot_nt(queries(), k_ref[0:META_BLOCK, :]) + _meta_key_bias()
    m_meta = jnp.broadcast_to(jnp.max(s_meta, axis=-1, keepdims=True), s_meta.shape)
    m_ref[...] = m_meta
    acc_ref[...] = _dot(jnp.exp2(s_meta - m_meta).astype(vext_ref.dtype), vext_ref[0:META_BLOCK, :])
    scores(0, 0)
    scores(1, 1)
    softmax(0)

    def body(i, carry):
        c = 2 * i
        scores(c + 2, 0)
        softmax(1)
        accumulate(c, 0)
        scores(c + 3, 1)
        softmax(0)
        accumulate(c + 1, 1)
        return carry

    lax.fori_loop(0, n_chunks // 2 - 1, body, 0)
    softmax(1)
    accumulate(n_chunks - 2, 0)
    accumulate(n_chunks - 1, 1)
    acc = acc_ref[...]
    o = acc[:, 0:HEAD_DIM] / acc[:, HEAD_DIM:]
    row = qb * tq + lax.broadcasted_iota(jnp.int32, (tq, 1), 0)
    keep = row >= META_OFF
    for g in range(group):
        o_ref[g] = jnp.where(keep, o[g * tq:(g + 1) * tq], 0.0).astype(o_ref.dtype)


def _gqa_attention(qkv, batch, n_q_heads, n_kv_heads):
    _, m, _ = qkv.shape
    lseg = m // batch
    group = n_q_heads // n_kv_heads
    tq = META_BLOCK
    nqb = lseg // tq
    assert (lseg - META_BLOCK) % (2 * GQA_TK) == 0
    n_chunks = (lseg - META_BLOCK) // GQA_TK
    kv_blk = (None, lseg, HEAD_DIM)
    q_blk = (group, tq, HEAD_DIM)
    mq = group * tq
    return pl.pallas_call(
        functools.partial(_gqa_kernel, group=group, tq=tq, n_chunks=n_chunks),
        grid=(batch, n_kv_heads, nqb),
        in_specs=[
            pl.BlockSpec(q_blk, lambda b, kv, i: (kv, b * nqb + i, 0)),
            pl.BlockSpec(kv_blk, lambda b, kv, i: (n_q_heads + kv, b, 0)),
            pl.BlockSpec(kv_blk, lambda b, kv, i: (n_q_heads + n_kv_heads + kv, b, 0)),
        ],
        out_specs=pl.BlockSpec(q_blk, lambda b, kv, i: (kv, b * nqb + i, 0)),
        out_shape=jax.ShapeDtypeStruct((n_q_heads, m, HEAD_DIM), COMPUTE_DTYPE),
        scratch_shapes=[
            pltpu.VMEM((lseg, 2 * HEAD_DIM), COMPUTE_DTYPE),
            pltpu.VMEM((mq, GQA_TK), F32),
            pltpu.VMEM((mq, GQA_TK), F32),
            pltpu.VMEM((mq, GQA_TK), COMPUTE_DTYPE),
            pltpu.VMEM((mq, GQA_TK), COMPUTE_DTYPE),
            pltpu.VMEM((mq, HEAD_DIM), F32),
            pltpu.VMEM((mq, HEAD_DIM), F32),
            pltpu.VMEM((mq, HEAD_DIM), F32),
            pltpu.VMEM((mq, 2 * HEAD_DIM), F32),
        ],
        compiler_params=_params(("arbitrary", "arbitrary", "arbitrary")),
        name="gqa_attention",
    )(qkv, qkv, qkv)


def _out_proj_kernel(o_ref, w_ref, h_ref, out_ref, *, n_heads):
    o = jnp.concatenate([o_ref[hh] for hh in range(n_heads)], axis=1)
    out_ref[...] = h_ref[...] + _dot(o, w_ref[...])


def _out_proj(o, w, h):
    n_heads, m, _ = o.shape
    d = h.shape[1]
    tm = _row_tile(m) // 2 if _row_tile(m) >= 256 else _row_tile(m)
    return pl.pallas_call(
        functools.partial(_out_proj_kernel, n_heads=n_heads),
        grid=(m // tm,),
        in_specs=[
            pl.BlockSpec((n_heads, tm, HEAD_DIM), lambda i: (0, i, 0)),
            pl.BlockSpec(w.shape, lambda i: (0, 0)),
            pl.BlockSpec((tm, d), lambda i: (i, 0)),
        ],
        out_specs=pl.BlockSpec((tm, d), lambda i: (i, 0)),
        out_shape=jax.ShapeDtypeStruct((m, d), F32),
        compiler_params=_params(("arbitrary",)),
        name="out_proj",
    )(o, w, h)


def _ffn_kernel(hp_ref, h_ref, hn_ref, g_ref, wg_ref, wv_ref, cw_ref, cb_ref, wo_ref, out_ref,
                xn_ref, gate_ref, acc_ref, *, tm):
    i = pl.program_id(0)
    f = pl.program_id(1)

    @pl.when(f == 0)
    def _():
        gain = g_ref[...]
        prev = jnp.where(i > 0, _rms(hp_ref[...], gain), 0.0)
        nxt = jnp.where(i < pl.num_programs(0) - 1, _rms(hn_ref[...], gain), 0.0)
        xn_ref[0:HALO, :] = prev.astype(xn_ref.dtype)
        xn_ref[HALO:HALO + tm, :] = _rms(h_ref[...], gain).astype(xn_ref.dtype)
        xn_ref[HALO + tm:, :] = nxt.astype(xn_ref.dtype)
        acc_ref[...] = jnp.zeros_like(acc_ref)

    gate_ref[...] = _dot(xn_ref[...], wg_ref[...])
    val = _dot(xn_ref[HALO:HALO + tm, :], wv_ref[...])
    gc = cb_ref[...]
    for tap in range(CONV_W):
        lo = HALO - CONV_W // 2 + tap
        gc = gc + gate_ref[lo:lo + tm, :] * cw_ref[tap:tap + 1, :]
    a = gc * (1.0 / (1.0 + jnp.exp(-gc))) * val
    acc_ref[...] += _dot(a.astype(wo_ref.dtype), wo_ref[...])

    @pl.when(f == pl.num_programs(1) - 1)
    def _():
        out_ref[...] = h_ref[...] + acc_ref[...]


def _ffn(h, gain, w_in, conv_w, conv_b, w_out):
    m, d = h.shape
    d_ff = w_out.shape[0]
    tm = _row_tile(m)
    tf = _col_tile(d_ff)
    n_i = m // tm
    hb = tm // HALO
    return pl.pallas_call(
        functools.partial(_ffn_kernel, tm=tm),
        grid=(n_i, d_ff // tf),
        in_specs=[
            pl.BlockSpec((HALO, d), lambda i, f: (jnp.maximum(i * hb - 1, 0), 0)),
            pl.BlockSpec((tm, d), lambda i, f: (i, 0)),
            pl.BlockSpec((HALO, d), lambda i, f: (jnp.minimum((i + 1) * hb, n_i * hb - 1), 0)),
            pl.BlockSpec((1, d), lambda i, f: (0, 0)),
            pl.BlockSpec((d, tf), lambda i, f: (0, f)),
            pl.BlockSpec((d, tf), lambda i, f: (0, d_ff // tf + f)),
            pl.BlockSpec((CONV_W, tf), lambda i, f: (0, f)),
            pl.BlockSpec((1, tf), lambda i, f: (0, f)),
            pl.BlockSpec((tf, d), lambda i, f: (f, 0)),
        ],
        out_specs=pl.BlockSpec((tm, d), lambda i, f: (i, 0)),
        out_shape=jax.ShapeDtypeStruct((m, d), F32),
        scratch_shapes=[
            pltpu.VMEM((tm + 2 * HALO, d), COMPUTE_DTYPE),
            pltpu.VMEM((tm + 2 * HALO, tf), F32),
            pltpu.VMEM((tm, d), F32),
        ],
        compiler_params=_params(("arbitrary", "arbitrary")),
        name="conv_glu",
    )(h, h, h, gain.reshape(1, d), w_in, w_in, conv_w, conv_b.reshape(1, d_ff), w_out)


def _final_norm_kernel(h_ref, g_ref, o_ref):
    o_ref[...] = _rms(h_ref[...], g_ref[...])


def _final_norm(h, gain, batch, t):
    m, d = h.shape
    blocks_per_seq = (m // batch) // META_BLOCK
    tt = META_BLOCK
    return pl.pallas_call(
        _final_norm_kernel,
        grid=(batch, t // tt),
        in_specs=[
            pl.BlockSpec((tt, d), lambda b, i: (b * blocks_per_seq + 1 + i, 0)),
            pl.BlockSpec((1, d), lambda b, i: (0, 0)),
        ],
        out_specs=pl.BlockSpec((None, tt, d), lambda b, i: (b, i, 0)),
        out_shape=jax.ShapeDtypeStruct((batch, t, d), F32),
        compiler_params=_params(("arbitrary", "arbitrary")),
        name="final_norm",
    )(h, gain.reshape(1, d))


def _rope_tables(batch, t):
    tok = jnp.arange(t)
    row = (tok // GRID_W).astype(F32)
    col = (tok % GRID_W).astype(F32)
    n_axis_pairs = HEAD_DIM // 4
    inv_freq = ROPE_THETA ** (-jnp.arange(n_axis_pairs, dtype=F32) / n_axis_pairs)
    ang = jnp.concatenate([row[:, None] * inv_freq[None], col[:, None] * inv_freq[None]], axis=-1)
    ang = jnp.concatenate([jnp.zeros((META_BLOCK, HEAD_DIM // 2), F32), ang], axis=0)
    cos, sin = jnp.cos(ang), jnp.sin(ang)
    cos = jnp.concatenate([cos, cos], axis=-1)
    sin = jnp.concatenate([-sin, sin], axis=-1)
    return jnp.tile(cos, (batch, 1)), jnp.tile(sin, (batch, 1))


def _trunk(x, meta_tokens, w):
    batch, t, d = x.shape
    assert t % GRID_W == 0 and d % HEAD_DIM == 0
    rows = t // GRID_W
    n_heads = d // HEAD_DIM
    scale = HEAD_DIM ** -0.5
    meta = jnp.broadcast_to(meta_tokens.astype(x.dtype)[None], (batch, N_META, d))
    h = jnp.concatenate([jnp.zeros((batch, META_OFF, d), x.dtype), meta, x], axis=1)
    h = h.reshape(batch * (META_BLOCK + t), d)
    depth = w["ffn_norm"].shape[0]
    rope = None
    for i in range(depth):
        j = i // 2
        if i % 2 == 0:
            qkv = _norm_proj(h, w["a_norm"][j], w["a_w_qkv"][j], n_heads, n_heads, scale)
            o = _na_attention(qkv, _na_bias_table(w["a_rpb"][j]), batch, rows, n_heads)
            h = _out_proj(o, w["a_w_o"][j], h)
        else:
            n_kv = (w["b_w_qkv"].shape[-1] // HEAD_DIM - n_heads) // 2
            if rope is None:
                rope = _rope_tables(batch, t)
            qkv = _norm_proj(h, w["b_norm"][j], w["b_w_qkv"][j], n_heads, n_kv, scale * LOG2E,
                             qk=(w["b_q_norm"][j], w["b_k_norm"][j]) + rope)
            o = _gqa_attention(qkv, batch, n_heads, n_kv)
            h = _out_proj(o, w["b_w_o"][j], h)
        h = _ffn(h, w["ffn_norm"][i], w["ffn_w_in"][i], w["ffn_conv_w"][i], w["ffn_conv_b"][i],
                 w["ffn_w_out"][i])
    return _final_norm(h, w["final_norm"], batch, t)


def kernel(x_prompt, x_sample, meta_tokens, a_norm, a_w_qkv, a_rpb, a_w_o, b_norm, b_w_qkv, b_q_norm,
           b_k_norm, b_w_o, ffn_norm, ffn_w_in, ffn_conv_w, ffn_conv_b, ffn_w_out, final_norm):
    cast = lambda a: a.astype(COMPUTE_DTYPE)
    w = dict(a_norm=a_norm, a_w_qkv=cast(a_w_qkv), a_rpb=a_rpb, a_w_o=cast(a_w_o), b_norm=b_norm,
             b_w_qkv=cast(b_w_qkv), b_q_norm=b_q_norm, b_k_norm=b_k_norm, b_w_o=cast(b_w_o),
             ffn_norm=ffn_norm, ffn_w_in=cast(ffn_w_in), ffn_conv_w=ffn_conv_w, ffn_conv_b=ffn_conv_b,
             ffn_w_out=cast(ffn_w_out), final_norm=final_norm)
    return (_trunk(x_prompt, meta_tokens, w), _trunk(x_sample, meta_tokens, w))
```

```python
import functools

import numpy as np
import jax
import jax.numpy as jnp
from jax import lax
from jax.experimental import pallas as pl
from jax.experimental.pallas import tpu as pltpu

HEAD_DIM = 128
N_META = 16
GRID_W = 64
NA_WIN_ROWS = 8
NA_WIN_COLS = 16
ROPE_THETA = 10000.0
EPS = 1e-6
CONV_W = 3

META_BLOCK = 128
META_OFF = META_BLOCK - N_META
NA_QROWS = 4
NA_KROWS = 12
HALO = 16
GQA_TK = 512
GQA_UNROLL = 6
NEG = -1e30
LOG2E = 1.4426950408889634
COMPUTE_DTYPE = jnp.bfloat16
V7X_VMEM_LIMIT_BYTES = 60 * 1024 * 1024

F32 = jnp.float32


def _dot(a, b):
    return jnp.dot(a, b, preferred_element_type=F32)


def _dot_nt(a, b):
    return lax.dot_general(a, b, (((1,), (1,)), ((), ())), preferred_element_type=F32)


def _rms(x, gain):
    ms = jnp.mean(x * x, axis=-1, keepdims=True)
    return x * lax.rsqrt(ms + EPS) * gain


def _params(semantics):
    return pltpu.CompilerParams(dimension_semantics=semantics, vmem_limit_bytes=V7X_VMEM_LIMIT_BYTES)


def _row_tile(m):
    for t in (768, 512, 384, 256, 128):
        if m % t == 0:
            return t
    raise ValueError(f"row count {m} is not a multiple of 128")


def _col_tile(n):
    for t in (512, 256, 128):
        if n % t == 0:
            return t
    raise ValueError(f"column count {n} is not a multiple of 128")


def _norm_proj_kernel(*refs, n_q_tiles, n_k_tiles, heads_per_tile, qk_norm, scale):
    if qk_norm:
        x_ref, g_ref, w_ref, qg_ref, kg_ref, cos_ref, sin_ref, o_ref, xn_ref = refs
    else:
        x_ref, g_ref, w_ref, o_ref, xn_ref = refs
    j = pl.program_id(1)

    @pl.when(j == 0)
    def _():
        xn_ref[...] = _rms(x_ref[...], g_ref[...]).astype(xn_ref.dtype)

    acc = _dot(xn_ref[...], w_ref[...])
    is_q = j < n_q_tiles
    tile_scale = jnp.where(is_q, scale, 1.0).astype(F32)

    def store(fn):
        for hh in range(heads_per_tile):
            o_ref[hh] = fn(acc[:, hh * HEAD_DIM:(hh + 1) * HEAD_DIM]).astype(o_ref.dtype)

    if not qk_norm:
        store(lambda y: y * tile_scale)
    else:
        is_qk = j < n_q_tiles + n_k_tiles
        gain = jnp.where(is_q, qg_ref[...], kg_ref[...])

        def rotary(y):
            yn = _rms(y, gain)
            return (yn * cos_ref[...] + pltpu.roll(yn, HEAD_DIM // 2, axis=1) * sin_ref[...]) * tile_scale

        @pl.when(is_qk)
        def _():
            store(rotary)

        @pl.when(jnp.logical_not(is_qk))
        def _():
            store(lambda y: y)


def _norm_proj(h, gain, w, n_q_heads, n_k_heads, scale, qk=None):
    m, d = h.shape
    n = w.shape[1]
    tm = _row_tile(m)
    tn = 4 * HEAD_DIM if (n_k_heads * HEAD_DIM) % (4 * HEAD_DIM) == 0 else HEAD_DIM
    hpt = tn // HEAD_DIM
    assert n % tn == 0 and (n_q_heads * HEAD_DIM) % tn == 0 and (n_k_heads * HEAD_DIM) % tn == 0
    kern = functools.partial(
        _norm_proj_kernel, n_q_tiles=n_q_heads // hpt, n_k_tiles=n_k_heads // hpt,
        heads_per_tile=hpt, qk_norm=qk is not None, scale=scale)
    in_specs = [
        pl.BlockSpec((tm, d), lambda i, j: (i, 0)),
        pl.BlockSpec((1, d), lambda i, j: (0, 0)),
        pl.BlockSpec((d, tn), lambda i, j: (0, j)),
    ]
    args = [h, gain.reshape(1, d), w]
    if qk is not None:
        q_gain, k_gain, cos, sin = qk
        in_specs += [
            pl.BlockSpec((1, HEAD_DIM), lambda i, j: (0, 0)),
            pl.BlockSpec((1, HEAD_DIM), lambda i, j: (0, 0)),
            pl.BlockSpec((tm, HEAD_DIM), lambda i, j: (i, 0)),
            pl.BlockSpec((tm, HEAD_DIM), lambda i, j: (i, 0)),
        ]
        args += [q_gain.reshape(1, HEAD_DIM), k_gain.reshape(1, HEAD_DIM), cos, sin]
    return pl.pallas_call(
        kern,
        grid=(m // tm, n // tn),
        in_specs=in_specs,
        out_specs=pl.BlockSpec((hpt, tm, HEAD_DIM), lambda i, j: (j, i, 0)),
        out_shape=jax.ShapeDtypeStruct((n // HEAD_DIM, m, HEAD_DIM), COMPUTE_DTYPE),
        scratch_shapes=[pltpu.VMEM((tm, d), COMPUTE_DTYPE)],
        compiler_params=_params(("arbitrary", "arbitrary")),
        name="norm_proj_rope" if qk is not None else "norm_proj",
    )(*args)


def _meta_key_bias():
    col = lax.broadcasted_iota(jnp.int32, (1, META_BLOCK), 1)
    return jnp.where(col >= META_OFF, 0.0, NEG).astype(F32)


def _na_kernel(q_ref, k_ref, v_ref, bias_ref, o_ref, vext_ref, s0_ref, s1_ref, p0_ref, p1_ref, *, rows):
    n_blocks = rows // NA_QROWS
    tq = NA_QROWS * GRID_W
    tk = NA_KROWS * GRID_W
    s_refs, p_refs = (s0_ref, s1_ref), (p0_ref, p1_ref)
    meta_bias = _meta_key_bias()

    vext_ref[:, 0:HEAD_DIM] = v_ref[...]
    vext_ref[:, HEAD_DIM:] = jnp.ones((vext_ref.shape[0], HEAD_DIM), vext_ref.dtype)

    k0 = k_ref[0:META_BLOCK, :]
    sm = _dot_nt(q_ref[0:META_BLOCK, :], k0) + meta_bias
    pm = jnp.exp(sm - jnp.max(sm, axis=-1, keepdims=True))
    om = _dot(pm.astype(vext_ref.dtype), vext_ref[0:META_BLOCK, :])
    row = lax.broadcasted_iota(jnp.int32, (META_BLOCK, 1), 0)
    o_ref[0:META_BLOCK, :] = jnp.where(
        row >= META_OFF, om[:, 0:HEAD_DIM] / om[:, HEAD_DIM:], 0.0).astype(o_ref.dtype)

    def query_start(i):
        return pl.multiple_of(META_BLOCK + i * tq, META_BLOCK)

    def window_start(i):
        w0 = jnp.clip(i * NA_QROWS - NA_WIN_ROWS // 2, 0, rows - NA_KROWS)
        return pl.multiple_of(META_BLOCK + w0 * GRID_W, META_BLOCK)

    def scores(i, buf):
        variant = jnp.where(i == 0, 0, jnp.where(i == n_blocks - 1, 2, 1))
        q = q_ref[pl.ds(query_start(i), tq), :]
        s_refs[buf][:, 0:META_BLOCK] = _dot_nt(q, k_ref[0:META_BLOCK, :]) + meta_bias
        s_refs[buf][:, META_BLOCK:] = _dot_nt(q, k_ref[pl.ds(window_start(i), tk), :]) + bias_ref[variant]

    def softmax(buf):
        n_tiles = (META_BLOCK + tk) // HEAD_DIM
        tiles = [s_refs[buf][:, t * HEAD_DIM:(t + 1) * HEAD_DIM] for t in range(n_tiles)]
        m = jnp.max(functools.reduce(jnp.maximum, tiles), axis=-1, keepdims=True)
        for t in range(n_tiles):
            p_refs[buf][:, t * HEAD_DIM:(t + 1) * HEAD_DIM] = jnp.exp(tiles[t] - m).astype(p_refs[buf].dtype)

    def output(i, buf):
        o = (_dot(p_refs[buf][:, 0:META_BLOCK], vext_ref[0:META_BLOCK, :])
             + _dot(p_refs[buf][:, META_BLOCK:], vext_ref[pl.ds(window_start(i), tk), :]))
        o_ref[pl.ds(query_start(i), tq), :] = (o[:, 0:HEAD_DIM] / o[:, HEAD_DIM:]).astype(o_ref.dtype)

    scores(0, 0)
    scores(1, 1)
    softmax(0)

    def body(t, carry):
        i = 2 * t
        scores(i + 2, 0)
        softmax(1)
        output(i, 0)
        scores(i + 3, 1)
        softmax(0)
        output(i + 1, 1)
        return carry

    lax.fori_loop(0, n_blocks // 2 - 1, body, 0)
    softmax(1)
    output(n_blocks - 2, 0)
    output(n_blocks - 1, 1)


def _na_bias_table(rpb):
    qi = np.arange(NA_QROWS)[:, None]
    kj = np.arange(NA_KROWS)[None, :]
    half = NA_WIN_ROWS // 2
    row_lo = [np.zeros_like(qi), qi, np.full_like(qi, half)]
    row_off = [0, -half, -NA_WIN_ROWS]
    ridx, rvalid = [], []
    for lo, off in zip(row_lo, row_off):
        rvalid.append((kj >= lo) & (kj < lo + NA_WIN_ROWS))
        ridx.append(kj + off - qi + (NA_WIN_ROWS - 1))
    ridx = np.stack(ridx)
    rvalid = np.stack(rvalid)
    cols = np.arange(GRID_W)
    cstart = np.clip(cols - NA_WIN_COLS // 2, 0, GRID_W - NA_WIN_COLS)[:, None]
    kc = cols[None, :]
    cvalid = (kc >= cstart) & (kc < cstart + NA_WIN_COLS)
    cidx = kc - cols[:, None] + (NA_WIN_COLS - 1)
    valid = rvalid[:, :, None, :, None] & cvalid[None, None, :, None, :]
    n_r, n_c = rpb.shape[1], rpb.shape[2]
    rsel = (np.where(rvalid, ridx, 0)[..., None] == np.arange(n_r)).astype(np.float32)
    csel = (np.arange(n_c)[:, None, None] == np.where(cvalid, cidx, 0)[None]).astype(np.float32)
    hi = lax.Precision.HIGHEST
    by_row = jnp.einsum("hrc,vqjr->hvqjc", rpb.astype(F32), rsel, precision=hi)
    bias = jnp.einsum("hvqjc,cxy->hvqxjy", by_row, csel, precision=hi)
    bias = jnp.where(valid[None], bias, NEG)
    return bias.reshape(rpb.shape[0], 3, NA_QROWS * GRID_W, NA_KROWS * GRID_W)


def _na_attention(qkv, bias, batch, seq_rows, n_heads):
    _, m, _ = qkv.shape
    lseg = m // batch
    rows = seq_rows
    assert rows % (2 * NA_QROWS) == 0 and rows >= NA_KROWS
    blk = (None, lseg, HEAD_DIM)
    tq, tk = NA_QROWS * GRID_W, META_BLOCK + NA_KROWS * GRID_W
    return pl.pallas_call(
        functools.partial(_na_kernel, rows=rows),
        grid=(batch, n_heads),
        in_specs=[
            pl.BlockSpec(blk, lambda b, h: (h, b, 0)),
            pl.BlockSpec(blk, lambda b, h: (n_heads + h, b, 0)),
            pl.BlockSpec(blk, lambda b, h: (2 * n_heads + h, b, 0)),
            pl.BlockSpec((None,) + bias.shape[1:], lambda b, h: (h, 0, 0, 0)),
        ],
        out_specs=pl.BlockSpec(blk, lambda b, h: (h, b, 0)),
        out_shape=jax.ShapeDtypeStruct((n_heads, m, HEAD_DIM), COMPUTE_DTYPE),
        scratch_shapes=[
            pltpu.VMEM((lseg, 2 * HEAD_DIM), COMPUTE_DTYPE),
            pltpu.VMEM((tq, tk), F32),
            pltpu.VMEM((tq, tk), F32),
            pltpu.VMEM((tq, tk), COMPUTE_DTYPE),
            pltpu.VMEM((tq, tk), COMPUTE_DTYPE),
        ],
        compiler_params=_params(("arbitrary", "arbitrary")),
        name="na_attention",
    )(qkv, qkv, qkv, bias)


def _gqa_kernel(q_ref, k_ref, v_ref, o_ref, vext_ref, s0_ref, s1_ref, p0_ref, p1_ref,
                a0_ref, a1_ref, m_ref, sm_ref, acc_ref, *, group, tq, n_chunks):
    qb = pl.program_id(2)
    n_tiles = GQA_TK // HEAD_DIM
    s_refs, p_refs, a_refs = (s0_ref, s1_ref), (p0_ref, p1_ref), (a0_ref, a1_ref)

    def queries():
        return jnp.concatenate([q_ref[g] for g in range(group)], axis=0)

    def chunk_start(c):
        return pl.multiple_of(META_BLOCK + c * GQA_TK, META_BLOCK)

    def scores(c, buf):
        s_refs[buf][...] = _dot_nt(queries(), k_ref[pl.ds(chunk_start(c), GQA_TK), :])

    @pl.when(qb == 0)
    def _():
        vext_ref[:, 0:HEAD_DIM] = v_ref[...]
        vext_ref[:, HEAD_DIM:] = jnp.ones((vext_ref.shape[0], HEAD_DIM), vext_ref.dtype)

    scores(0, 0)
    scores(1, 1)

    def softmax(buf):
        for g in range(group):
            r0 = g * tq
            tiles = [s_refs[buf][r0:r0 + tq, t * HEAD_DIM:(t + 1) * HEAD_DIM] for t in range(n_tiles)]
            m_prev = m_ref[r0:r0 + tq, :]
            m_cur = functools.reduce(jnp.maximum, tiles)
            m_new = jnp.maximum(m_prev, jnp.max(m_cur, axis=-1, keepdims=True))
            m_ref[r0:r0 + tq, :] = m_new
            a_refs[buf][r0:r0 + tq, :] = jnp.exp2(m_prev - m_new)
            for t in range(n_tiles):
                p_refs[buf][r0:r0 + tq, t * HEAD_DIM:(t + 1) * HEAD_DIM] = (
                    jnp.exp2(tiles[t] - m_new).astype(p_refs[buf].dtype))

    def accumulate(c, buf):
        pv = _dot(p_refs[buf][...], vext_ref[pl.ds(chunk_start(c), GQA_TK), :])
        alpha = a_refs[buf][...]
        acc_ref[...] = acc_ref[...] * jnp.concatenate([alpha, alpha], axis=1) + pv

    m_ref[...] = jnp.full(m_ref.shape, NEG, F32)
    acc_ref[...] = jnp.zeros_like(acc_ref)
    sm_ref[...] = _dot_nt(queries(), k_ref[0:META_BLOCK, :]) + _meta_key_bias()
    softmax(0)

    def steps(c, count):
        for d in range(count):
            scores(c + d + 2, d % 2)
            softmax((d + 1) % 2)
            accumulate(c + d, d % 2)

    main = n_chunks - 2
    unroll = next(u for u in (GQA_UNROLL, 4, 2) if main % u == 0 or u == 2)
    trips = main // unroll

    def body(i, carry):
        steps(i * unroll, unroll)
        return carry

    lax.fori_loop(0, trips, body, 0)
    steps(trips * unroll, main - trips * unroll)
    softmax(1)
    m_prev = m_ref[...]
    s_meta = sm_ref[...]
    m_new = jnp.maximum(m_prev, jnp.max(s_meta, axis=-1, keepdims=True))
    alpha = jnp.exp2(m_prev - m_new)
    pv_meta = _dot(jnp.exp2(s_meta - m_new).astype(vext_ref.dtype), vext_ref[0:META_BLOCK, :])
    accumulate(n_chunks - 2, 0)
    accumulate(n_chunks - 1, 1)
    acc = acc_ref[...] * jnp.concatenate([alpha, alpha], axis=1) + pv_meta
    o = acc[:, 0:HEAD_DIM] / acc[:, HEAD_DIM:]
    row = qb * tq + lax.broadcasted_iota(jnp.int32, (tq, 1), 0)
    keep = row >= META_OFF
    for g in range(group):
        o_ref[g] = jnp.where(keep, o[g * tq:(g + 1) * tq], 0.0).astype(o_ref.dtype)


def _gqa_attention(qkv, batch, n_q_heads, n_kv_heads):
    _, m, _ = qkv.shape
    lseg = m // batch
    group = n_q_heads // n_kv_heads
    tq = META_BLOCK
    nqb = lseg // tq
    assert (lseg - META_BLOCK) % (2 * GQA_TK) == 0
    n_chunks = (lseg - META_BLOCK) // GQA_TK
    kv_blk = (None, lseg, HEAD_DIM)
    q_blk = (group, tq, HEAD_DIM)
    mq = group * tq
    return pl.pallas_call(
        functools.partial(_gqa_kernel, group=group, tq=tq, n_chunks=n_chunks),
        grid=(batch, n_kv_heads, nqb),
        in_specs=[
            pl.BlockSpec(q_blk, lambda b, kv, i: (kv, b * nqb + i, 0)),
            pl.BlockSpec(kv_blk, lambda b, kv, i: (n_q_heads + kv, b, 0)),
            pl.BlockSpec(kv_blk, lambda b, kv, i: (n_q_heads + n_kv_heads + kv, b, 0)),
        ],
        out_specs=pl.BlockSpec(q_blk, lambda b, kv, i: (kv, b * nqb + i, 0)),
        out_shape=jax.ShapeDtypeStruct((n_q_heads, m, HEAD_DIM), COMPUTE_DTYPE),
        scratch_shapes=[
            pltpu.VMEM((lseg, 2 * HEAD_DIM), COMPUTE_DTYPE),
            pltpu.VMEM((mq, GQA_TK), F32),
            pltpu.VMEM((mq, GQA_TK), F32),
            pltpu.VMEM((mq, GQA_TK), COMPUTE_DTYPE),
            pltpu.VMEM((mq, GQA_TK), COMPUTE_DTYPE),
            pltpu.VMEM((mq, HEAD_DIM), F32),
            pltpu.VMEM((mq, HEAD_DIM), F32),
            pltpu.VMEM((mq, HEAD_DIM), F32),
            pltpu.VMEM((mq, META_BLOCK), F32),
            pltpu.VMEM((mq, 2 * HEAD_DIM), F32),
        ],
        compiler_params=_params(("arbitrary", "arbitrary", "arbitrary")),
        name="gqa_attention",
    )(qkv, qkv, qkv)


def _out_proj_kernel(o_ref, w_ref, h_ref, out_ref, *, n_heads):
    o = jnp.concatenate([o_ref[hh] for hh in range(n_heads)], axis=1)
    out_ref[...] = h_ref[...] + _dot(o, w_ref[...])


def _out_proj(o, w, h):
    n_heads, m, _ = o.shape
    d = h.shape[1]
    tm = _row_tile(m) // 2 if _row_tile(m) >= 256 else _row_tile(m)
    return pl.pallas_call(
        functools.partial(_out_proj_kernel, n_heads=n_heads),
        grid=(m // tm,),
        in_specs=[
            pl.BlockSpec((n_heads, tm, HEAD_DIM), lambda i: (0, i, 0)),
            pl.BlockSpec(w.shape, lambda i: (0, 0)),
            pl.BlockSpec((tm, d), lambda i: (i, 0)),
        ],
        out_specs=pl.BlockSpec((tm, d), lambda i: (i, 0)),
        out_shape=jax.ShapeDtypeStruct((m, d), F32),
        compiler_params=_params(("arbitrary",)),
        name="out_proj",
    )(o, w, h)


def _ffn_kernel(hp_ref, h_ref, hn_ref, g_ref, wg_ref, wv_ref, cw_ref, cb_ref, wo_ref, out_ref,
                xn_ref, gate_ref, acc_ref, *, tm):
    i = pl.program_id(0)
    f = pl.program_id(1)

    @pl.when(f == 0)
    def _():
        gain = g_ref[...]
        prev = jnp.where(i > 0, _rms(hp_ref[...], gain), 0.0)
        nxt = jnp.where(i < pl.num_programs(0) - 1, _rms(hn_ref[...], gain), 0.0)
        xn_ref[0:HALO, :] = prev.astype(xn_ref.dtype)
        xn_ref[HALO:HALO + tm, :] = _rms(h_ref[...], gain).astype(xn_ref.dtype)
        xn_ref[HALO + tm:, :] = nxt.astype(xn_ref.dtype)
        acc_ref[...] = jnp.zeros_like(acc_ref)

    gate_ref[...] = _dot(xn_ref[...], wg_ref[...])
    val = _dot(xn_ref[HALO:HALO + tm, :], wv_ref[...])
    gc = cb_ref[...]
    for tap in range(CONV_W):
        lo = HALO - CONV_W // 2 + tap
        gc = gc + gate_ref[lo:lo + tm, :] * cw_ref[tap:tap + 1, :]
    a = gc * (1.0 / (1.0 + jnp.exp(-gc))) * val
    acc_ref[...] += _dot(a.astype(wo_ref.dtype), wo_ref[...])

    @pl.when(f == pl.num_programs(1) - 1)
    def _():
        out_ref[...] = h_ref[...] + acc_ref[...]


def _ffn(h, gain, w_in, conv_w, conv_b, w_out):
    m, d = h.shape
    d_ff = w_out.shape[0]
    tm = _row_tile(m)
    tf = _col_tile(d_ff)
    n_i = m // tm
    hb = tm // HALO
    return pl.pallas_call(
        functools.partial(_ffn_kernel, tm=tm),
        grid=(n_i, d_ff // tf),
        in_specs=[
            pl.BlockSpec((HALO, d), lambda i, f: (jnp.maximum(i * hb - 1, 0), 0)),
            pl.BlockSpec((tm, d), lambda i, f: (i, 0)),
            pl.BlockSpec((HALO, d), lambda i, f: (jnp.minimum((i + 1) * hb, n_i * hb - 1), 0)),
            pl.BlockSpec((1, d), lambda i, f: (0, 0)),
            pl.BlockSpec((d, tf), lambda i, f: (0, f)),
            pl.BlockSpec((d, tf), lambda i, f: (0, d_ff // tf + f)),
            pl.BlockSpec((CONV_W, tf), lambda i, f: (0, f)),
            pl.BlockSpec((1, tf), lambda i, f: (0, f)),
            pl.BlockSpec((tf, d), lambda i, f: (f, 0)),
        ],
        out_specs=pl.BlockSpec((tm, d), lambda i, f: (i, 0)),
        out_shape=jax.ShapeDtypeStruct((m, d), F32),
        scratch_shapes=[
            pltpu.VMEM((tm + 2 * HALO, d), COMPUTE_DTYPE),
            pltpu.VMEM((tm + 2 * HALO, tf), F32),
            pltpu.VMEM((tm, d), F32),
        ],
        compiler_params=_params(("arbitrary", "arbitrary")),
        name="conv_glu",
    )(h, h, h, gain.reshape(1, d), w_in, w_in, conv_w, conv_b.reshape(1, d_ff), w_out)


def _final_norm_kernel(h_ref, g_ref, o_ref):
    o_ref[...] = _rms(h_ref[...], g_ref[...])


def _final_norm(h, gain, batch, t):
    m, d = h.shape
    lseg = m // batch
    tt = _row_tile(t)
    return pl.pallas_call(
        _final_norm_kernel,
        grid=(batch, t // tt),
        in_specs=[
            pl.BlockSpec((pl.Element(tt), pl.Element(d)),
                         lambda b, i: (pl.multiple_of(b * lseg + META_BLOCK + i * tt, META_BLOCK), 0)),
            pl.BlockSpec((1, d), lambda b, i: (0, 0)),
        ],
        out_specs=pl.BlockSpec((None, tt, d), lambda b, i: (b, i, 0)),
        out_shape=jax.ShapeDtypeStruct((batch, t, d), F32),
        compiler_params=_params(("arbitrary", "arbitrary")),
        name="final_norm",
    )(h, gain.reshape(1, d))


def _rope_tables(batch, t):
    tok = jnp.arange(t)
    row = (tok // GRID_W).astype(F32)
    col = (tok % GRID_W).astype(F32)
    n_axis_pairs = HEAD_DIM // 4
    inv_freq = ROPE_THETA ** (-jnp.arange(n_axis_pairs, dtype=F32) / n_axis_pairs)
    ang = jnp.concatenate([row[:, None] * inv_freq[None], col[:, None] * inv_freq[None]], axis=-1)
    ang = jnp.concatenate([jnp.zeros((META_BLOCK, HEAD_DIM // 2), F32), ang], axis=0)
    cos, sin = jnp.cos(ang), jnp.sin(ang)
    cos = jnp.concatenate([cos, cos], axis=-1)
    sin = jnp.concatenate([-sin, sin], axis=-1)
    return jnp.tile(cos, (batch, 1)), jnp.tile(sin, (batch, 1))


def _trunk(x, meta_tokens, w):
    batch, t, d = x.shape
    assert t % GRID_W == 0 and d % HEAD_DIM == 0
    rows = t // GRID_W
    n_heads = d // HEAD_DIM
    scale = HEAD_DIM ** -0.5
    meta = jnp.broadcast_to(meta_tokens.astype(x.dtype)[None], (batch, N_META, d))
    h = jnp.concatenate([jnp.zeros((batch, META_OFF, d), x.dtype), meta, x], axis=1)
    h = h.reshape(batch * (META_BLOCK + t), d)
    depth = w["ffn_norm"].shape[0]
    rope = None
    for i in range(depth):
        j = i // 2
        if i % 2 == 0:
            qkv = _norm_proj(h, w["a_norm"][j], w["a_w_qkv"][j], n_heads, n_heads, scale)
            o = _na_attention(qkv, _na_bias_table(w["a_rpb"][j]), batch, rows, n_heads)
            h = _out_proj(o, w["a_w_o"][j], h)
        else:
            n_kv = (w["b_w_qkv"].shape[-1] // HEAD_DIM - n_heads) // 2
            if rope is None:
                rope = _rope_tables(batch, t)
            qkv = _norm_proj(h, w["b_norm"][j], w["b_w_qkv"][j], n_heads, n_kv, scale * LOG2E,
                             qk=(w["b_q_norm"][j], w["b_k_norm"][j]) + rope)
            o = _gqa_attention(qkv, batch, n_heads, n_kv)
            h = _out_proj(o, w["b_w_o"][j], h)
        h = _ffn(h, w["ffn_norm"][i], w["ffn_w_in"][i], w["ffn_conv_w"][i], w["ffn_conv_b"][i],
                 w["ffn_w_out"][i])
    return _final_norm(h, w["final_norm"], batch, t)


def kernel(x_prompt, x_sample, meta_tokens, a_norm, a_w_qkv, a_rpb, a_w_o, b_norm, b_w_qkv, b_q_norm,
           b_k_norm, b_w_o, ffn_norm, ffn_w_in, ffn_conv_w, ffn_conv_b, ffn_w_out, final_norm):
    cast = lambda a: a.astype(COMPUTE_DTYPE)
    w = dict(a_norm=a_norm, a_w_qkv=cast(a_w_qkv), a_rpb=a_rpb, a_w_o=cast(a_w_o), b_norm=b_norm,
             b_w_qkv=cast(b_w_qkv), b_q_norm=b_q_norm, b_k_norm=b_k_norm, b_w_o=cast(b_w_o),
             ffn_norm=ffn_norm, ffn_w_in=cast(ffn_w_in), ffn_conv_w=ffn_conv_w, ffn_conv_b=ffn_conv_b,
             ffn_w_out=cast(ffn_w_out), final_norm=final_norm)
    return (_trunk(x_prompt, meta_tokens, w), _trunk(x_sample, meta_tokens, w))
```

```python
import functools

import numpy as np
import jax
import jax.numpy as jnp
from jax import lax
from jax.experimental import pallas as pl
from jax.experimental.pallas import tpu as pltpu

HEAD_DIM = 128
N_META = 16
GRID_W = 64
NA_WIN_ROWS = 8
NA_WIN_COLS = 16
ROPE_THETA = 10000.0
EPS = 1e-6
CONV_W = 3

META_BLOCK = 128
META_OFF = META_BLOCK - N_META
NA_QROWS = 4
NA_KROWS = 12
HALO = 16
GQA_TK = 512
GQA_UNROLL = 10
GQA_BOUND_SLACK = 1.0 + 2.0 ** -6
GQA_BOUND_LIMIT = 40.0
NEG = -1e30
LOG2E = 1.4426950408889634
COMPUTE_DTYPE = jnp.bfloat16
V7X_VMEM_LIMIT_BYTES = 60 * 1024 * 1024

F32 = jnp.float32


def _dot(a, b):
    return jnp.dot(a, b, preferred_element_type=F32)


def _dot_nt(a, b):
    return lax.dot_general(a, b, (((1,), (1,)), ((), ())), preferred_element_type=F32)


def _rms(x, gain):
    ms = jnp.mean(x * x, axis=-1, keepdims=True)
    return x * lax.rsqrt(ms + EPS) * gain


def _params(semantics):
    return pltpu.CompilerParams(dimension_semantics=semantics, vmem_limit_bytes=V7X_VMEM_LIMIT_BYTES)


def _row_tile(m):
    for t in (768, 512, 384, 256, 128):
        if m % t == 0:
            return t
    raise ValueError(f"row count {m} is not a multiple of 128")


def _col_tile(n):
    for t in (512, 256, 128):
        if n % t == 0:
            return t
    raise ValueError(f"column count {n} is not a multiple of 128")


def _norm_proj_kernel(*refs, n_q_tiles, n_k_tiles, heads_per_tile, qk_norm, scale):
    if qk_norm:
        x_ref, g_ref, w_ref, qg_ref, kg_ref, cos_ref, sin_ref, o_ref, xn_ref = refs
    else:
        x_ref, g_ref, w_ref, o_ref, xn_ref = refs
    j = pl.program_id(1)

    @pl.when(j == 0)
    def _():
        xn_ref[...] = _rms(x_ref[...], g_ref[...]).astype(xn_ref.dtype)

    acc = _dot(xn_ref[...], w_ref[...])
    is_q = j < n_q_tiles
    tile_scale = jnp.where(is_q, scale, 1.0).astype(F32)

    def store(fn):
        for hh in range(heads_per_tile):
            o_ref[hh] = fn(acc[:, hh * HEAD_DIM:(hh + 1) * HEAD_DIM]).astype(o_ref.dtype)

    if not qk_norm:
        store(lambda y: y * tile_scale)
    else:
        is_qk = j < n_q_tiles + n_k_tiles
        gain = jnp.where(is_q, qg_ref[...], kg_ref[...])

        def rotary(y):
            yn = _rms(y, gain)
            return (yn * cos_ref[...] + pltpu.roll(yn, HEAD_DIM // 2, axis=1) * sin_ref[...]) * tile_scale

        @pl.when(is_qk)
        def _():
            store(rotary)

        @pl.when(jnp.logical_not(is_qk))
        def _():
            store(lambda y: y)


def _norm_proj(h, gain, w, n_q_heads, n_k_heads, scale, qk=None):
    m, d = h.shape
    n = w.shape[1]
    tm = _row_tile(m)
    tn = next(t for t in (8 * HEAD_DIM, 4 * HEAD_DIM, HEAD_DIM) if (n_k_heads * HEAD_DIM) % t == 0)
    hpt = tn // HEAD_DIM
    assert n % tn == 0 and (n_q_heads * HEAD_DIM) % tn == 0 and (n_k_heads * HEAD_DIM) % tn == 0
    kern = functools.partial(
        _norm_proj_kernel, n_q_tiles=n_q_heads // hpt, n_k_tiles=n_k_heads // hpt,
        heads_per_tile=hpt, qk_norm=qk is not None, scale=scale)
    in_specs = [
        pl.BlockSpec((tm, d), lambda i, j: (i, 0)),
        pl.BlockSpec((1, d), lambda i, j: (0, 0)),
        pl.BlockSpec((d, tn), lambda i, j: (0, j)),
    ]
    args = [h, gain.reshape(1, d), w]
    if qk is not None:
        q_gain, k_gain, cos, sin = qk
        in_specs += [
            pl.BlockSpec((1, HEAD_DIM), lambda i, j: (0, 0)),
            pl.BlockSpec((1, HEAD_DIM), lambda i, j: (0, 0)),
            pl.BlockSpec((tm, HEAD_DIM), lambda i, j: (i, 0)),
            pl.BlockSpec((tm, HEAD_DIM), lambda i, j: (i, 0)),
        ]
        args += [q_gain.reshape(1, HEAD_DIM), k_gain.reshape(1, HEAD_DIM), cos, sin]
    return pl.pallas_call(
        kern,
        grid=(m // tm, n // tn),
        in_specs=in_specs,
        out_specs=pl.BlockSpec((hpt, tm, HEAD_DIM), lambda i, j: (j, i, 0)),
        out_shape=jax.ShapeDtypeStruct((n // HEAD_DIM, m, HEAD_DIM), COMPUTE_DTYPE),
        scratch_shapes=[pltpu.VMEM((tm, d), COMPUTE_DTYPE)],
        compiler_params=_params(("arbitrary", "arbitrary")),
        name="norm_proj_rope" if qk is not None else "norm_proj",
    )(*args)


def _meta_key_bias():
    col = lax.broadcasted_iota(jnp.int32, (1, META_BLOCK), 1)
    return jnp.where(col >= META_OFF, 0.0, NEG).astype(F32)


def _na_kernel(q_ref, k_ref, v_ref, bias_ref, o_ref, vext_ref, s0_ref, s1_ref, p0_ref, p1_ref, *, rows):
    n_blocks = rows // NA_QROWS
    tq = NA_QROWS * GRID_W
    tk = NA_KROWS * GRID_W
    s_refs, p_refs = (s0_ref, s1_ref), (p0_ref, p1_ref)
    meta_bias = _meta_key_bias()

    vext_ref[:, 0:HEAD_DIM] = v_ref[...]
    vext_ref[:, HEAD_DIM:] = jnp.ones((vext_ref.shape[0], HEAD_DIM), vext_ref.dtype)

    k0 = k_ref[0:META_BLOCK, :]
    sm = _dot_nt(q_ref[0:META_BLOCK, :], k0) + meta_bias
    pm = jnp.exp(sm - jnp.max(sm, axis=-1, keepdims=True))
    om = _dot(pm.astype(vext_ref.dtype), vext_ref[0:META_BLOCK, :])
    row = lax.broadcasted_iota(jnp.int32, (META_BLOCK, 1), 0)
    o_ref[0:META_BLOCK, :] = jnp.where(
        row >= META_OFF, om[:, 0:HEAD_DIM] / om[:, HEAD_DIM:], 0.0).astype(o_ref.dtype)

    def query_start(i):
        return pl.multiple_of(META_BLOCK + i * tq, META_BLOCK)

    def window_start(i):
        w0 = jnp.clip(i * NA_QROWS - NA_WIN_ROWS // 2, 0, rows - NA_KROWS)
        return pl.multiple_of(META_BLOCK + w0 * GRID_W, META_BLOCK)

    def scores(i, buf):
        variant = jnp.where(i == 0, 0, jnp.where(i == n_blocks - 1, 2, 1))
        q = q_ref[pl.ds(query_start(i), tq), :]
        s_refs[buf][:, 0:META_BLOCK] = _dot_nt(q, k_ref[0:META_BLOCK, :]) + meta_bias
        s_refs[buf][:, META_BLOCK:] = _dot_nt(q, k_ref[pl.ds(window_start(i), tk), :]) + bias_ref[variant]

    def softmax(buf):
        n_tiles = (META_BLOCK + tk) // HEAD_DIM
        tiles = [s_refs[buf][:, t * HEAD_DIM:(t + 1) * HEAD_DIM] for t in range(n_tiles)]
        m = jnp.max(functools.reduce(jnp.maximum, tiles), axis=-1, keepdims=True)
        for t in range(n_tiles):
            p_refs[buf][:, t * HEAD_DIM:(t + 1) * HEAD_DIM] = jnp.exp(tiles[t] - m).astype(p_refs[buf].dtype)

    def output(i, buf):
        o = (_dot(p_refs[buf][:, 0:META_BLOCK], vext_ref[0:META_BLOCK, :])
             + _dot(p_refs[buf][:, META_BLOCK:], vext_ref[pl.ds(window_start(i), tk), :]))
        o_ref[pl.ds(query_start(i), tq), :] = (o[:, 0:HEAD_DIM] / o[:, HEAD_DIM:]).astype(o_ref.dtype)

    scores(0, 0)
    scores(1, 1)
    softmax(0)

    def body(t, carry):
        i = 2 * t
        scores(i + 2, 0)
        softmax(1)
        output(i, 0)
        scores(i + 3, 1)
        softmax(0)
        output(i + 1, 1)
        return carry

    lax.fori_loop(0, n_blocks // 2 - 1, body, 0)
    softmax(1)
    output(n_blocks - 2, 0)
    output(n_blocks - 1, 1)


def _na_bias_table(rpb):
    qi = np.arange(NA_QROWS)[:, None]
    kj = np.arange(NA_KROWS)[None, :]
    half = NA_WIN_ROWS // 2
    row_lo = [np.zeros_like(qi), qi, np.full_like(qi, half)]
    row_off = [0, -half, -NA_WIN_ROWS]
    ridx, rvalid = [], []
    for lo, off in zip(row_lo, row_off):
        rvalid.append((kj >= lo) & (kj < lo + NA_WIN_ROWS))
        ridx.append(kj + off - qi + (NA_WIN_ROWS - 1))
    ridx = np.stack(ridx)
    rvalid = np.stack(rvalid)
    cols = np.arange(GRID_W)
    cstart = np.clip(cols - NA_WIN_COLS // 2, 0, GRID_W - NA_WIN_COLS)[:, None]
    kc = cols[None, :]
    cvalid = (kc >= cstart) & (kc < cstart + NA_WIN_COLS)
    cidx = kc - cols[:, None] + (NA_WIN_COLS - 1)
    valid = rvalid[:, :, None, :, None] & cvalid[None, None, :, None, :]
    n_r, n_c = rpb.shape[1], rpb.shape[2]
    rsel = (np.where(rvalid, ridx, 0)[..., None] == np.arange(n_r)).astype(np.float32)
    csel = (np.arange(n_c)[:, None, None] == np.where(cvalid, cidx, 0)[None]).astype(np.float32)
    hi = lax.Precision.HIGHEST
    by_row = jnp.einsum("hrc,vqjr->hvqjc", rpb.astype(F32), rsel, precision=hi)
    bias = jnp.einsum("hvqjc,cxy->hvqxjy", by_row, csel, precision=hi)
    bias = jnp.where(valid[None], bias, NEG)
    return bias.reshape(rpb.shape[0], 3, NA_QROWS * GRID_W, NA_KROWS * GRID_W)


def _na_attention(qkv, bias, batch, seq_rows, n_heads):
    _, m, _ = qkv.shape
    lseg = m // batch
    rows = seq_rows
    assert rows % (2 * NA_QROWS) == 0 and rows >= NA_KROWS
    blk = (None, lseg, HEAD_DIM)
    tq, tk = NA_QROWS * GRID_W, META_BLOCK + NA_KROWS * GRID_W
    return pl.pallas_call(
        functools.partial(_na_kernel, rows=rows),
        grid=(batch, n_heads),
        in_specs=[
            pl.BlockSpec(blk, lambda b, h: (h, b, 0)),
            pl.BlockSpec(blk, lambda b, h: (n_heads + h, b, 0)),
            pl.BlockSpec(blk, lambda b, h: (2 * n_heads + h, b, 0)),
            pl.BlockSpec((None,) + bias.shape[1:], lambda b, h: (h, 0, 0, 0)),
        ],
        out_specs=pl.BlockSpec(blk, lambda b, h: (h, b, 0)),
        out_shape=jax.ShapeDtypeStruct((n_heads, m, HEAD_DIM), COMPUTE_DTYPE),
        scratch_shapes=[
            pltpu.VMEM((lseg, 2 * HEAD_DIM), COMPUTE_DTYPE),
            pltpu.VMEM((tq, tk), F32),
            pltpu.VMEM((tq, tk), F32),
            pltpu.VMEM((tq, tk), COMPUTE_DTYPE),
            pltpu.VMEM((tq, tk), COMPUTE_DTYPE),
        ],
        compiler_params=_params(("arbitrary", "arbitrary")),
        name="na_attention",
    )(qkv, qkv, qkv, bias)


def _gqa_kernel(q_ref, k_ref, v_ref, o_ref, vext_ref, kmax_ref, s0_ref, s1_ref, p0_ref, p1_ref,
                a0_ref, a1_ref, m_ref, sm_ref, acc_ref, *, group, tq, n_chunks):
    qb = pl.program_id(2)
    n_tiles = GQA_TK // HEAD_DIM
    s_refs, p_refs, a_refs = (s0_ref, s1_ref), (p0_ref, p1_ref), (a0_ref, a1_ref)

    def queries():
        return jnp.concatenate([q_ref[g] for g in range(group)], axis=0)

    def chunk_start(c):
        return pl.multiple_of(META_BLOCK + c * GQA_TK, META_BLOCK)

    def scores(c, buf):
        s_refs[buf][...] = _dot_nt(queries(), k_ref[pl.ds(chunk_start(c), GQA_TK), :])

    @pl.when(qb == 0)
    def _():
        vext_ref[:, 0:HEAD_DIM] = v_ref[...]
        vext_ref[:, HEAD_DIM:] = jnp.ones((vext_ref.shape[0], HEAD_DIM), vext_ref.dtype)

        def key_norm(c, best):
            kc = k_ref[pl.ds(chunk_start(c), GQA_TK), :].astype(F32)
            return jnp.maximum(best, jnp.sum(kc * kc, axis=-1, keepdims=True))

        k0 = k_ref[0:META_BLOCK, :].astype(F32)
        best = lax.fori_loop(0, n_chunks, key_norm, jnp.zeros((GQA_TK, 1), F32))
        kmax2 = jnp.maximum(jnp.max(best), jnp.max(jnp.sum(k0 * k0, axis=-1, keepdims=True)))
        kmax_ref[...] = jnp.full(kmax_ref.shape, jnp.sqrt(kmax2), F32)

    qf = queries().astype(F32)
    bound = (jnp.sqrt(jnp.sum(qf * qf, axis=-1, keepdims=True)) * kmax_ref[0:1, 0:1]) * GQA_BOUND_SLACK
    use_bound = jnp.max(bound) <= GQA_BOUND_LIMIT

    def softmax(buf, fixed):
        for g in range(group):
            r0 = g * tq
            tiles = [s_refs[buf][r0:r0 + tq, t * HEAD_DIM:(t + 1) * HEAD_DIM] for t in range(n_tiles)]
            if fixed:
                m_new = m_ref[r0:r0 + tq, :]
            else:
                m_prev = m_ref[r0:r0 + tq, :]
                m_cur = functools.reduce(jnp.maximum, tiles)
                m_new = jnp.maximum(m_prev, jnp.max(m_cur, axis=-1, keepdims=True))
                m_ref[r0:r0 + tq, :] = m_new
                a_refs[buf][r0:r0 + tq, :] = jnp.exp2(m_prev - m_new)
            for t in range(n_tiles):
                p_refs[buf][r0:r0 + tq, t * HEAD_DIM:(t + 1) * HEAD_DIM] = (
                    jnp.exp2(tiles[t] - m_new).astype(p_refs[buf].dtype))

    def accumulate(c, buf, fixed):
        pv = _dot(p_refs[buf][...], vext_ref[pl.ds(chunk_start(c), GQA_TK), :])
        if fixed:
            acc_ref[...] = acc_ref[...] + pv
        else:
            alpha = a_refs[buf][...]
            acc_ref[...] = acc_ref[...] * jnp.concatenate([alpha, alpha], axis=1) + pv

    def attend(fixed):
        scores(0, 0)
        scores(1, 1)
        if fixed:
            m_ref[...] = jnp.broadcast_to(bound, m_ref.shape)
        else:
            m_ref[...] = jnp.full(m_ref.shape, NEG, F32)
        acc_ref[...] = jnp.zeros_like(acc_ref)
        sm_ref[...] = _dot_nt(queries(), k_ref[0:META_BLOCK, :]) + _meta_key_bias()
        softmax(0, fixed)

        def steps(c, count):
            for d in range(count):
                scores(c + d + 2, d % 2)
                softmax((d + 1) % 2, fixed)
                accumulate(c + d, d % 2, fixed)

        main = n_chunks - 2
        unroll = next(u for u in (GQA_UNROLL, 4, 2) if main % u == 0 or u == 2)
        trips = main // unroll

        def body(i, carry):
            steps(i * unroll, unroll)
            return carry

        lax.fori_loop(0, trips, body, 0)
        steps(trips * unroll, main - trips * unroll)
        softmax(1, fixed)
        m_prev = m_ref[...]
        s_meta = sm_ref[...]
        if fixed:
            m_new = m_prev
        else:
            m_new = jnp.maximum(m_prev, jnp.max(s_meta, axis=-1, keepdims=True))
            alpha = jnp.exp2(m_prev - m_new)
        pv_meta = _dot(jnp.exp2(s_meta - m_new).astype(vext_ref.dtype), vext_ref[0:META_BLOCK, :])
        accumulate(n_chunks - 2, 0, fixed)
        accumulate(n_chunks - 1, 1, fixed)
        if fixed:
            acc = acc_ref[...] + pv_meta
        else:
            acc = acc_ref[...] * jnp.concatenate([alpha, alpha], axis=1) + pv_meta
        o = acc[:, 0:HEAD_DIM] / acc[:, HEAD_DIM:]
        row = qb * tq + lax.broadcasted_iota(jnp.int32, (tq, 1), 0)
        keep = row >= META_OFF
        for g in range(group):
            o_ref[g] = jnp.where(keep, o[g * tq:(g + 1) * tq], 0.0).astype(o_ref.dtype)

    @pl.when(use_bound)
    def _():
        attend(True)

    @pl.when(jnp.logical_not(use_bound))
    def _():
        attend(False)


def _gqa_attention(qkv, batch, n_q_heads, n_kv_heads):
    _, m, _ = qkv.shape
    lseg = m // batch
    group = n_q_heads // n_kv_heads
    tq = META_BLOCK
    nqb = lseg // tq
    assert (lseg - META_BLOCK) % (2 * GQA_TK) == 0
    n_chunks = (lseg - META_BLOCK) // GQA_TK
    kv_blk = (None, lseg, HEAD_DIM)
    q_blk = (group, tq, HEAD_DIM)
    mq = group * tq
    return pl.pallas_call(
        functools.partial(_gqa_kernel, group=group, tq=tq, n_chunks=n_chunks),
        grid=(batch, n_kv_heads, nqb),
        in_specs=[
            pl.BlockSpec(q_blk, lambda b, kv, i: (kv, b * nqb + i, 0)),
            pl.BlockSpec(kv_blk, lambda b, kv, i: (n_q_heads + kv, b, 0)),
            pl.BlockSpec(kv_blk, lambda b, kv, i: (n_q_heads + n_kv_heads + kv, b, 0)),
        ],
        out_specs=pl.BlockSpec(q_blk, lambda b, kv, i: (kv, b * nqb + i, 0)),
        out_shape=jax.ShapeDtypeStruct((n_q_heads, m, HEAD_DIM), COMPUTE_DTYPE),
        scratch_shapes=[
            pltpu.VMEM((lseg, 2 * HEAD_DIM), COMPUTE_DTYPE),
            pltpu.VMEM((8, HEAD_DIM), F32),
            pltpu.VMEM((mq, GQA_TK), F32),
            pltpu.VMEM((mq, GQA_TK), F32),
            pltpu.VMEM((mq, GQA_TK), COMPUTE_DTYPE),
            pltpu.VMEM((mq, GQA_TK), COMPUTE_DTYPE),
            pltpu.VMEM((mq, HEAD_DIM), F32),
            pltpu.VMEM((mq, HEAD_DIM), F32),
            pltpu.VMEM((mq, HEAD_DIM), F32),
            pltpu.VMEM((mq, META_BLOCK), F32),
            pltpu.VMEM((mq, 2 * HEAD_DIM), F32),
        ],
        compiler_params=_params(("arbitrary", "arbitrary", "arbitrary")),
        name="gqa_attention",
    )(qkv, qkv, qkv)


def _out_proj_kernel(o_ref, w_ref, h_ref, out_ref, *, n_heads):
    o = jnp.concatenate([o_ref[hh] for hh in range(n_heads)], axis=1)
    out_ref[...] = h_ref[...] + _dot(o, w_ref[...])


def _out_proj(o, w, h):
    n_heads, m, _ = o.shape
    d = h.shape[1]
    tm = _row_tile(m) // 2 if _row_tile(m) >= 256 else _row_tile(m)
    return pl.pallas_call(
        functools.partial(_out_proj_kernel, n_heads=n_heads),
        grid=(m // tm,),
        in_specs=[
            pl.BlockSpec((n_heads, tm, HEAD_DIM), lambda i: (0, i, 0)),
            pl.BlockSpec(w.shape, lambda i: (0, 0)),
            pl.BlockSpec((tm, d), lambda i: (i, 0)),
        ],
        out_specs=pl.BlockSpec((tm, d), lambda i: (i, 0)),
        out_shape=jax.ShapeDtypeStruct((m, d), F32),
        compiler_params=_params(("arbitrary",)),
        name="out_proj",
    )(o, w, h)


def _ffn_kernel(hp_ref, h_ref, hn_ref, g_ref, wg_ref, wv_ref, cw_ref, cb_ref, wo_ref, out_ref,
                xn_ref, gate_ref, acc_ref, *, tm):
    i = pl.program_id(0)
    f = pl.program_id(1)

    @pl.when(f == 0)
    def _():
        gain = g_ref[...]
        prev = jnp.where(i > 0, _rms(hp_ref[...], gain), 0.0)
        nxt = jnp.where(i < pl.num_programs(0) - 1, _rms(hn_ref[...], gain), 0.0)
        xn_ref[0:HALO, :] = prev.astype(xn_ref.dtype)
        xn_ref[HALO:HALO + tm, :] = _rms(h_ref[...], gain).astype(xn_ref.dtype)
        xn_ref[HALO + tm:, :] = nxt.astype(xn_ref.dtype)
        acc_ref[...] = jnp.zeros_like(acc_ref)

    gate_ref[...] = _dot(xn_ref[...], wg_ref[...])
    val = _dot(xn_ref[HALO:HALO + tm, :], wv_ref[...])
    gc = cb_ref[...]
    for tap in range(CONV_W):
        lo = HALO - CONV_W // 2 + tap
        gc = gc + gate_ref[lo:lo + tm, :] * cw_ref[tap:tap + 1, :]
    a = gc * (1.0 / (1.0 + jnp.exp(-gc))) * val
    acc_ref[...] += _dot(a.astype(wo_ref.dtype), wo_ref[...])

    @pl.when(f == pl.num_programs(1) - 1)
    def _():
        out_ref[...] = h_ref[...] + acc_ref[...]


def _ffn(h, gain, w_in, conv_w, conv_b, w_out):
    m, d = h.shape
    d_ff = w_out.shape[0]
    tm = _row_tile(m)
    tf = _col_tile(d_ff)
    n_i = m // tm
    hb = tm // HALO
    return pl.pallas_call(
        functools.partial(_ffn_kernel, tm=tm),
        grid=(n_i, d_ff // tf),
        in_specs=[
            pl.BlockSpec((HALO, d), lambda i, f: (jnp.maximum(i * hb - 1, 0), 0)),
            pl.BlockSpec((tm, d), lambda i, f: (i, 0)),
            pl.BlockSpec((HALO, d), lambda i, f: (jnp.minimum((i + 1) * hb, n_i * hb - 1), 0)),
            pl.BlockSpec((1, d), lambda i, f: (0, 0)),
            pl.BlockSpec((d, tf), lambda i, f: (0, f)),
            pl.BlockSpec((d, tf), lambda i, f: (0, d_ff // tf + f)),
            pl.BlockSpec((CONV_W, tf), lambda i, f: (0, f)),
            pl.BlockSpec((1, tf), lambda i, f: (0, f)),
            pl.BlockSpec((tf, d), lambda i, f: (f, 0)),
        ],
        out_specs=pl.BlockSpec((tm, d), lambda i, f: (i, 0)),
        out_shape=jax.ShapeDtypeStruct((m, d), F32),
        scratch_shapes=[
            pltpu.VMEM((tm + 2 * HALO, d), COMPUTE_DTYPE),
            pltpu.VMEM((tm + 2 * HALO, tf), F32),
            pltpu.VMEM((tm, d), F32),
        ],
        compiler_params=_params(("arbitrary", "arbitrary")),
        name="conv_glu",
    )(h, h, h, gain.reshape(1, d), w_in, w_in, conv_w, conv_b.reshape(1, d_ff), w_out)


def _final_norm_kernel(h_ref, g_ref, o_ref):
    o_ref[...] = _rms(h_ref[...], g_ref[...])


def _final_norm(h, gain, batch, t):
    m, d = h.shape
    lseg = m // batch
    tt = _row_tile(t)
    return pl.pallas_call(
        _final_norm_kernel,
        grid=(batch, t // tt),
        in_specs=[
            pl.BlockSpec((pl.Element(tt), pl.Element(d)),
                         lambda b, i: (pl.multiple_of(b * lseg + META_BLOCK + i * tt, META_BLOCK), 0)),
            pl.BlockSpec((1, d), lambda b, i: (0, 0)),
        ],
        out_specs=pl.BlockSpec((None, tt, d), lambda b, i: (b, i, 0)),
        out_shape=jax.ShapeDtypeStruct((batch, t, d), F32),
        compiler_params=_params(("arbitrary", "arbitrary")),
        name="final_norm",
    )(h, gain.reshape(1, d))


def _rope_tables(batch, t):
    tok = jnp.arange(t)
    row = (tok // GRID_W).astype(F32)
    col = (tok % GRID_W).astype(F32)
    n_axis_pairs = HEAD_DIM // 4
    inv_freq = ROPE_THETA ** (-jnp.arange(n_axis_pairs, dtype=F32) / n_axis_pairs)
    ang = jnp.concatenate([row[:, None] * inv_freq[None], col[:, None] * inv_freq[None]], axis=-1)
    ang = jnp.concatenate([jnp.zeros((META_BLOCK, HEAD_DIM // 2), F32), ang], axis=0)
    cos, sin = jnp.cos(ang), jnp.sin(ang)
    cos = jnp.concatenate([cos, cos], axis=-1)
    sin = jnp.concatenate([-sin, sin], axis=-1)
    return jnp.tile(cos, (batch, 1)), jnp.tile(sin, (batch, 1))


def _trunk(x, meta_tokens, w):
    batch, t, d = x.shape
    assert t % GRID_W == 0 and d % HEAD_DIM == 0
    rows = t // GRID_W
    n_heads = d // HEAD_DIM
    scale = HEAD_DIM ** -0.5
    meta = jnp.broadcast_to(meta_tokens.astype(x.dtype)[None], (batch, N_META, d))
    h = jnp.concatenate([jnp.zeros((batch, META_OFF, d), x.dtype), meta, x], axis=1)
    h = h.reshape(batch * (META_BLOCK + t), d)
    depth = w["ffn_norm"].shape[0]
    rope = None
    for i in range(depth):
        j = i // 2
        if i % 2 == 0:
            qkv = _norm_proj(h, w["a_norm"][j], w["a_w_qkv"][j], n_heads, n_heads, scale)
            o = _na_attention(qkv, _na_bias_table(w["a_rpb"][j]), batch, rows, n_heads)
            h = _out_proj(o, w["a_w_o"][j], h)
        else:
            n_kv = (w["b_w_qkv"][j].shape[-1] // HEAD_DIM - n_heads) // 2
            if rope is None:
                rope = _rope_tables(batch, t)
            qkv = _norm_proj(h, w["b_norm"][j], w["b_w_qkv"][j], n_heads, n_kv, scale * LOG2E,
                             qk=(w["b_q_norm"][j], w["b_k_norm"][j]) + rope)
            o = _gqa_attention(qkv, batch, n_heads, n_kv)
            h = _out_proj(o, w["b_w_o"][j], h)
        h = _ffn(h, w["ffn_norm"][i], w["ffn_w_in"][i], w["ffn_conv_w"][i], w["ffn_conv_b"][i],
                 w["ffn_w_out"][i])
    return _final_norm(h, w["final_norm"], batch, t)


def kernel(x_prompt, x_sample, meta_tokens, a_norm, a_w_qkv, a_rpb, a_w_o, b_norm, b_w_qkv, b_q_norm,
           b_k_norm, b_w_o, ffn_norm, ffn_w_in, ffn_conv_w, ffn_conv_b, ffn_w_out, final_norm):
    cast = lambda a: [a[i].astype(COMPUTE_DTYPE) for i in range(a.shape[0])]
    w = dict(a_norm=a_norm, a_w_qkv=cast(a_w_qkv), a_rpb=a_rpb, a_w_o=cast(a_w_o), b_norm=b_norm,
             b_w_qkv=cast(b_w_qkv), b_q_norm=b_q_norm, b_k_norm=b_k_norm, b_w_o=cast(b_w_o),
             ffn_norm=ffn_norm, ffn_w_in=cast(ffn_w_in), ffn_conv_w=ffn_conv_w, ffn_conv_b=ffn_conv_b,
             ffn_w_out=cast(ffn_w_out), final_norm=final_norm)
    return (_trunk(x_prompt, meta_tokens, w), _trunk(x_sample, meta_tokens, w))
```

```python
import functools

import numpy as np
import jax
import jax.numpy as jnp
from jax import lax
from jax.experimental import pallas as pl
from jax.experimental.pallas import tpu as pltpu

HEAD_DIM = 128
N_META = 16
GRID_W = 64
NA_WIN_ROWS = 8
NA_WIN_COLS = 16
ROPE_THETA = 10000.0
EPS = 1e-6
CONV_W = 3

META_BLOCK = 128
META_OFF = META_BLOCK - N_META
NA_QROWS = 4
NA_KROWS = 12
HALO = 16
GQA_TK = 512
GQA_UNROLL = 10
GQA_BOUND_SLACK = 1.0 + 2.0 ** -6
GQA_BOUND_LIMIT = 40.0
NEG = -1e30
LOG2E = 1.4426950408889634
COMPUTE_DTYPE = jnp.bfloat16
V7X_VMEM_LIMIT_BYTES = 60 * 1024 * 1024

F32 = jnp.float32


def _dot(a, b):
    return jnp.dot(a, b, preferred_element_type=F32)


def _dot_nt(a, b):
    return lax.dot_general(a, b, (((1,), (1,)), ((), ())), preferred_element_type=F32)


def _rms(x, gain):
    ms = jnp.mean(x * x, axis=-1, keepdims=True)
    return x * lax.rsqrt(ms + EPS) * gain


def _params(semantics):
    return pltpu.CompilerParams(dimension_semantics=semantics, vmem_limit_bytes=V7X_VMEM_LIMIT_BYTES)


def _row_tile(m):
    for t in (768, 512, 384, 256, 128):
        if m % t == 0:
            return t
    raise ValueError(f"row count {m} is not a multiple of 128")


def _col_tile(n):
    for t in (512, 256, 128):
        if n % t == 0:
            return t
    raise ValueError(f"column count {n} is not a multiple of 128")


def _norm_proj_kernel(*refs, n_q_tiles, n_k_tiles, heads_per_tile, qk_norm, scale):
    if qk_norm:
        x_ref, g_ref, w_ref, qg_ref, kg_ref, cos_ref, sin_ref, o_ref, xn_ref = refs
    else:
        x_ref, g_ref, w_ref, o_ref, xn_ref = refs
    j = pl.program_id(1)

    @pl.when(j == 0)
    def _():
        xn_ref[...] = _rms(x_ref[...], g_ref[...]).astype(xn_ref.dtype)

    acc = _dot(xn_ref[...], w_ref[...])
    is_q = j < n_q_tiles
    tile_scale = jnp.where(is_q, scale, 1.0).astype(F32)

    def store(fn):
        for hh in range(heads_per_tile):
            o_ref[hh] = fn(acc[:, hh * HEAD_DIM:(hh + 1) * HEAD_DIM]).astype(o_ref.dtype)

    if not qk_norm:
        store(lambda y: y * tile_scale)
    else:
        is_qk = j < n_q_tiles + n_k_tiles
        gain = jnp.where(is_q, qg_ref[...], kg_ref[...])

        def rotary(y):
            yn = _rms(y, gain)
            return (yn * cos_ref[...] + pltpu.roll(yn, HEAD_DIM // 2, axis=1) * sin_ref[...]) * tile_scale

        @pl.when(is_qk)
        def _():
            store(rotary)

        @pl.when(jnp.logical_not(is_qk))
        def _():
            store(lambda y: y)


def _norm_proj(h, gain, w, n_q_heads, n_k_heads, scale, qk=None):
    m, d = h.shape
    n = w.shape[1]
    tm = _row_tile(m)
    tn = next(t for t in (8 * HEAD_DIM, 4 * HEAD_DIM, HEAD_DIM) if (n_k_heads * HEAD_DIM) % t == 0)
    hpt = tn // HEAD_DIM
    assert n % tn == 0 and (n_q_heads * HEAD_DIM) % tn == 0 and (n_k_heads * HEAD_DIM) % tn == 0
    kern = functools.partial(
        _norm_proj_kernel, n_q_tiles=n_q_heads // hpt, n_k_tiles=n_k_heads // hpt,
        heads_per_tile=hpt, qk_norm=qk is not None, scale=scale)
    in_specs = [
        pl.BlockSpec((tm, d), lambda i, j: (i, 0)),
        pl.BlockSpec((1, d), lambda i, j: (0, 0)),
        pl.BlockSpec((d, tn), lambda i, j: (0, j)),
    ]
    args = [h, gain.reshape(1, d), w]
    if qk is not None:
        q_gain, k_gain, cos, sin = qk
        in_specs += [
            pl.BlockSpec((1, HEAD_DIM), lambda i, j: (0, 0)),
            pl.BlockSpec((1, HEAD_DIM), lambda i, j: (0, 0)),
            pl.BlockSpec((tm, HEAD_DIM), lambda i, j: (i, 0)),
            pl.BlockSpec((tm, HEAD_DIM), lambda i, j: (i, 0)),
        ]
        args += [q_gain.reshape(1, HEAD_DIM), k_gain.reshape(1, HEAD_DIM), cos, sin]
    return pl.pallas_call(
        kern,
        grid=(m // tm, n // tn),
        in_specs=in_specs,
        out_specs=pl.BlockSpec((hpt, tm, HEAD_DIM), lambda i, j: (j, i, 0)),
        out_shape=jax.ShapeDtypeStruct((n // HEAD_DIM, m, HEAD_DIM), COMPUTE_DTYPE),
        scratch_shapes=[pltpu.VMEM((tm, d), COMPUTE_DTYPE)],
        compiler_params=_params(("arbitrary", "arbitrary")),
        name="norm_proj_rope" if qk is not None else "norm_proj",
    )(*args)


def _meta_key_bias():
    col = lax.broadcasted_iota(jnp.int32, (1, META_BLOCK), 1)
    return jnp.where(col >= META_OFF, 0.0, NEG).astype(F32)


def _na_kernel(q_ref, k_ref, v_ref, bias_ref, o_ref, vext_ref, s0_ref, s1_ref, p0_ref, p1_ref, *, rows):
    n_blocks = rows // NA_QROWS
    tq = NA_QROWS * GRID_W
    tk = NA_KROWS * GRID_W
    s_refs, p_refs = (s0_ref, s1_ref), (p0_ref, p1_ref)
    meta_bias = _meta_key_bias()

    vext_ref[:, 0:HEAD_DIM] = v_ref[...]
    vext_ref[:, HEAD_DIM:] = jnp.ones((vext_ref.shape[0], HEAD_DIM), vext_ref.dtype)

    k0 = k_ref[0:META_BLOCK, :]
    sm = _dot_nt(q_ref[0:META_BLOCK, :], k0) + meta_bias
    pm = jnp.exp(sm - jnp.max(sm, axis=-1, keepdims=True))
    om = _dot(pm.astype(vext_ref.dtype), vext_ref[0:META_BLOCK, :])
    row = lax.broadcasted_iota(jnp.int32, (META_BLOCK, 1), 0)
    o_ref[0:META_BLOCK, :] = jnp.where(
        row >= META_OFF, om[:, 0:HEAD_DIM] / om[:, HEAD_DIM:], 0.0).astype(o_ref.dtype)

    def query_start(i):
        return pl.multiple_of(META_BLOCK + i * tq, META_BLOCK)

    def window_start(i):
        w0 = jnp.clip(i * NA_QROWS - NA_WIN_ROWS // 2, 0, rows - NA_KROWS)
        return pl.multiple_of(META_BLOCK + w0 * GRID_W, META_BLOCK)

    def scores(i, buf):
        variant = jnp.where(i == 0, 0, jnp.where(i == n_blocks - 1, 2, 1))
        q = q_ref[pl.ds(query_start(i), tq), :]
        s_refs[buf][:, 0:META_BLOCK] = _dot_nt(q, k_ref[0:META_BLOCK, :]) + meta_bias
        s_refs[buf][:, META_BLOCK:] = _dot_nt(q, k_ref[pl.ds(window_start(i), tk), :]) + bias_ref[variant]

    def softmax(buf):
        n_tiles = (META_BLOCK + tk) // HEAD_DIM
        tiles = [s_refs[buf][:, t * HEAD_DIM:(t + 1) * HEAD_DIM] for t in range(n_tiles)]
        m = jnp.max(functools.reduce(jnp.maximum, tiles), axis=-1, keepdims=True)
        for t in range(n_tiles):
            p_refs[buf][:, t * HEAD_DIM:(t + 1) * HEAD_DIM] = jnp.exp(tiles[t] - m).astype(p_refs[buf].dtype)

    def output(i, buf):
        o = (_dot(p_refs[buf][:, 0:META_BLOCK], vext_ref[0:META_BLOCK, :])
             + _dot(p_refs[buf][:, META_BLOCK:], vext_ref[pl.ds(window_start(i), tk), :]))
        o_ref[pl.ds(query_start(i), tq), :] = (o[:, 0:HEAD_DIM] / o[:, HEAD_DIM:]).astype(o_ref.dtype)

    scores(0, 0)
    scores(1, 1)
    softmax(0)

    def body(t, carry):
        i = 2 * t
        scores(i + 2, 0)
        softmax(1)
        output(i, 0)
        scores(i + 3, 1)
        softmax(0)
        output(i + 1, 1)
        return carry

    lax.fori_loop(0, n_blocks // 2 - 1, body, 0)
    softmax(1)
    output(n_blocks - 2, 0)
    output(n_blocks - 1, 1)


def _na_bias_table(rpb):
    qi = np.arange(NA_QROWS)[:, None]
    kj = np.arange(NA_KROWS)[None, :]
    half = NA_WIN_ROWS // 2
    row_lo = [np.zeros_like(qi), qi, np.full_like(qi, half)]
    row_off = [0, -half, -NA_WIN_ROWS]
    ridx, rvalid = [], []
    for lo, off in zip(row_lo, row_off):
        rvalid.append((kj >= lo) & (kj < lo + NA_WIN_ROWS))
        ridx.append(kj + off - qi + (NA_WIN_ROWS - 1))
    ridx = np.stack(ridx)
    rvalid = np.stack(rvalid)
    cols = np.arange(GRID_W)
    cstart = np.clip(cols - NA_WIN_COLS // 2, 0, GRID_W - NA_WIN_COLS)[:, None]
    kc = cols[None, :]
    cvalid = (kc >= cstart) & (kc < cstart + NA_WIN_COLS)
    cidx = kc - cols[:, None] + (NA_WIN_COLS - 1)
    valid = rvalid[:, :, None, :, None] & cvalid[None, None, :, None, :]
    n_r, n_c = rpb.shape[1], rpb.shape[2]
    rsel = (np.where(rvalid, ridx, 0)[..., None] == np.arange(n_r)).astype(np.float32)
    csel = (np.arange(n_c)[:, None, None] == np.where(cvalid, cidx, 0)[None]).astype(np.float32)
    hi = lax.Precision.HIGHEST
    by_row = jnp.einsum("hrc,vqjr->hvqjc", rpb.astype(F32), rsel, precision=hi)
    bias = jnp.einsum("hvqjc,cxy->hvqxjy", by_row, csel, precision=hi)
    bias = jnp.where(valid[None], bias, NEG)
    return bias.reshape(rpb.shape[0], 3, NA_QROWS * GRID_W, NA_KROWS * GRID_W)


def _na_attention(qkv, bias, batch, seq_rows, n_heads):
    _, m, _ = qkv.shape
    lseg = m // batch
    rows = seq_rows
    assert rows % (2 * NA_QROWS) == 0 and rows >= NA_KROWS
    blk = (None, lseg, HEAD_DIM)
    tq, tk = NA_QROWS * GRID_W, META_BLOCK + NA_KROWS * GRID_W
    return pl.pallas_call(
        functools.partial(_na_kernel, rows=rows),
        grid=(batch, n_heads),
        in_specs=[
            pl.BlockSpec(blk, lambda b, h: (h, b, 0)),
            pl.BlockSpec(blk, lambda b, h: (n_heads + h, b, 0)),
            pl.BlockSpec(blk, lambda b, h: (2 * n_heads + h, b, 0)),
            pl.BlockSpec((None,) + bias.shape[1:], lambda b, h: (h, 0, 0, 0)),
        ],
        out_specs=pl.BlockSpec(blk, lambda b, h: (h, b, 0)),
        out_shape=jax.ShapeDtypeStruct((n_heads, m, HEAD_DIM), COMPUTE_DTYPE),
        scratch_shapes=[
            pltpu.VMEM((lseg, 2 * HEAD_DIM), COMPUTE_DTYPE),
            pltpu.VMEM((tq, tk), F32),
            pltpu.VMEM((tq, tk), F32),
            pltpu.VMEM((tq, tk), COMPUTE_DTYPE),
            pltpu.VMEM((tq, tk), COMPUTE_DTYPE),
        ],
        compiler_params=_params(("arbitrary", "arbitrary")),
        name="na_attention",
    )(qkv, qkv, qkv, bias)


def _gqa_kernel(q_ref, k_ref, v_ref, o_ref, vext_ref, kmax_ref, s0_ref, s1_ref, p0_ref, p1_ref,
                a0_ref, a1_ref, m_ref, sm_ref, acc_ref, *, group, tq, n_sub, n_chunks):
    qb = pl.program_id(2)
    n_tiles = GQA_TK // HEAD_DIM
    s_refs, p_refs, a_refs = (s0_ref, s1_ref), (p0_ref, p1_ref), (a0_ref, a1_ref)

    def queries(sub):
        return jnp.concatenate([q_ref[g, sub * tq:(sub + 1) * tq, :] for g in range(group)], axis=0)

    def chunk_start(c):
        return pl.multiple_of(META_BLOCK + c * GQA_TK, META_BLOCK)

    def scores(sub, c, buf):
        s_refs[buf][...] = _dot_nt(queries(sub), k_ref[pl.ds(chunk_start(c), GQA_TK), :])

    @pl.when(qb == 0)
    def _():
        vext_ref[:, 0:HEAD_DIM] = v_ref[...]
        vext_ref[:, HEAD_DIM:] = jnp.ones((vext_ref.shape[0], HEAD_DIM), vext_ref.dtype)

        def key_norm(c, best):
            kc = k_ref[pl.ds(chunk_start(c), GQA_TK), :].astype(F32)
            return jnp.maximum(best, jnp.sum(kc * kc, axis=-1, keepdims=True))

        k0 = k_ref[0:META_BLOCK, :].astype(F32)
        best = lax.fori_loop(0, n_chunks, key_norm, jnp.zeros((GQA_TK, 1), F32))
        kmax2 = jnp.maximum(jnp.max(best), jnp.max(jnp.sum(k0 * k0, axis=-1, keepdims=True)))
        kmax_ref[...] = jnp.full(kmax_ref.shape, jnp.sqrt(kmax2), F32)

    bounds = []
    for sub in range(n_sub):
        qf = queries(sub).astype(F32)
        bounds.append((jnp.sqrt(jnp.sum(qf * qf, axis=-1, keepdims=True)) * kmax_ref[0:1, 0:1])
                      * GQA_BOUND_SLACK)
    use_bound = functools.reduce(jnp.maximum, [jnp.max(b) for b in bounds]) <= GQA_BOUND_LIMIT

    def softmax(buf, fixed):
        for g in range(group):
            r0 = g * tq
            tiles = [s_refs[buf][r0:r0 + tq, t * HEAD_DIM:(t + 1) * HEAD_DIM] for t in range(n_tiles)]
            if fixed:
                m_new = m_ref[r0:r0 + tq, :]
            else:
                m_prev = m_ref[r0:r0 + tq, :]
                m_cur = functools.reduce(jnp.maximum, tiles)
                m_new = jnp.maximum(m_prev, jnp.max(m_cur, axis=-1, keepdims=True))
                m_ref[r0:r0 + tq, :] = m_new
                a_refs[buf][r0:r0 + tq, :] = jnp.exp2(m_prev - m_new)
            for t in range(n_tiles):
                p_refs[buf][r0:r0 + tq, t * HEAD_DIM:(t + 1) * HEAD_DIM] = (
                    jnp.exp2(tiles[t] - m_new).astype(p_refs[buf].dtype))

    def accumulate(c, buf, fixed):
        pv = _dot(p_refs[buf][...], vext_ref[pl.ds(chunk_start(c), GQA_TK), :])
        if fixed:
            acc_ref[...] = acc_ref[...] + pv
        else:
            alpha = a_refs[buf][...]
            acc_ref[...] = acc_ref[...] * jnp.concatenate([alpha, alpha], axis=1) + pv

    def attend(sub, fixed):
        bound = bounds[sub]
        scores(sub, 0, 0)
        scores(sub, 1, 1)
        if fixed:
            m_ref[...] = jnp.broadcast_to(bound, m_ref.shape)
        else:
            m_ref[...] = jnp.full(m_ref.shape, NEG, F32)
        acc_ref[...] = jnp.zeros_like(acc_ref)
        sm_ref[...] = _dot_nt(queries(sub), k_ref[0:META_BLOCK, :]) + _meta_key_bias()
        softmax(0, fixed)

        def steps(c, count):
            for d in range(count):
                scores(sub, c + d + 2, d % 2)
                softmax((d + 1) % 2, fixed)
                accumulate(c + d, d % 2, fixed)

        main = n_chunks - 2
        unroll = next(u for u in (GQA_UNROLL, 4, 2) if main % u == 0 or u == 2)
        trips = main // unroll

        def body(i, carry):
            steps(i * unroll, unroll)
            return carry

        lax.fori_loop(0, trips, body, 0)
        steps(trips * unroll, main - trips * unroll)
        softmax(1, fixed)
        m_prev = m_ref[...]
        s_meta = sm_ref[...]
        if fixed:
            m_new = m_prev
        else:
            m_new = jnp.maximum(m_prev, jnp.max(s_meta, axis=-1, keepdims=True))
            alpha = jnp.exp2(m_prev - m_new)
        pv_meta = _dot(jnp.exp2(s_meta - m_new).astype(vext_ref.dtype), vext_ref[0:META_BLOCK, :])
        accumulate(n_chunks - 2, 0, fixed)
        accumulate(n_chunks - 1, 1, fixed)
        if fixed:
            acc = acc_ref[...] + pv_meta
        else:
            acc = acc_ref[...] * jnp.concatenate([alpha, alpha], axis=1) + pv_meta
        o = acc[:, 0:HEAD_DIM] / acc[:, HEAD_DIM:]
        row = (qb * n_sub + sub) * tq + lax.broadcasted_iota(jnp.int32, (tq, 1), 0)
        keep = row >= META_OFF
        for g in range(group):
            o_ref[g, sub * tq:(sub + 1) * tq, :] = jnp.where(
                keep, o[g * tq:(g + 1) * tq], 0.0).astype(o_ref.dtype)

    @pl.when(use_bound)
    def _():
        for sub in range(n_sub):
            attend(sub, True)

    @pl.when(jnp.logical_not(use_bound))
    def _():
        for sub in range(n_sub):
            attend(sub, False)


def _gqa_attention(qkv, batch, n_q_heads, n_kv_heads):
    _, m, _ = qkv.shape
    lseg = m // batch
    group = n_q_heads // n_kv_heads
    tq = META_BLOCK
    n_sub = next(n for n in (3, 2, 1) if (lseg // tq) % n == 0)
    nqb = lseg // (tq * n_sub)
    assert (lseg - META_BLOCK) % (2 * GQA_TK) == 0
    n_chunks = (lseg - META_BLOCK) // GQA_TK
    kv_blk = (None, lseg, HEAD_DIM)
    q_blk = (group, tq * n_sub, HEAD_DIM)
    mq = group * tq
    return pl.pallas_call(
        functools.partial(_gqa_kernel, group=group, tq=tq, n_sub=n_sub, n_chunks=n_chunks),
        grid=(batch, n_kv_heads, nqb),
        in_specs=[
            pl.BlockSpec(q_blk, lambda b, kv, i: (kv, b * nqb + i, 0)),
            pl.BlockSpec(kv_blk, lambda b, kv, i: (n_q_heads + kv, b, 0)),
            pl.BlockSpec(kv_blk, lambda b, kv, i: (n_q_heads + n_kv_heads + kv, b, 0)),
        ],
        out_specs=pl.BlockSpec(q_blk, lambda b, kv, i: (kv, b * nqb + i, 0)),
        out_shape=jax.ShapeDtypeStruct((n_q_heads, m, HEAD_DIM), COMPUTE_DTYPE),
        scratch_shapes=[
            pltpu.VMEM((lseg, 2 * HEAD_DIM), COMPUTE_DTYPE),
            pltpu.VMEM((8, HEAD_DIM), F32),
            pltpu.VMEM((mq, GQA_TK), F32),
            pltpu.VMEM((mq, GQA_TK), F32),
            pltpu.VMEM((mq, GQA_TK), COMPUTE_DTYPE),
            pltpu.VMEM((mq, GQA_TK), COMPUTE_DTYPE),
            pltpu.VMEM((mq, HEAD_DIM), F32),
            pltpu.VMEM((mq, HEAD_DIM), F32),
            pltpu.VMEM((mq, HEAD_DIM), F32),
            pltpu.VMEM((mq, META_BLOCK), F32),
            pltpu.VMEM((mq, 2 * HEAD_DIM), F32),
        ],
        compiler_params=_params(("arbitrary", "arbitrary", "arbitrary")),
        name="gqa_attention",
    )(qkv, qkv, qkv)


def _out_proj_kernel(o_ref, w_ref, h_ref, out_ref, *, n_heads):
    o = jnp.concatenate([o_ref[hh] for hh in range(n_heads)], axis=1)
    out_ref[...] = h_ref[...] + _dot(o, w_ref[...])


def _out_proj(o, w, h):
    n_heads, m, _ = o.shape
    d = h.shape[1]
    tm = _row_tile(m) // 2 if _row_tile(m) >= 256 else _row_tile(m)
    return pl.pallas_call(
        functools.partial(_out_proj_kernel, n_heads=n_heads),
        grid=(m // tm,),
        in_specs=[
            pl.BlockSpec((n_heads, tm, HEAD_DIM), lambda i: (0, i, 0)),
            pl.BlockSpec(w.shape, lambda i: (0, 0)),
            pl.BlockSpec((tm, d), lambda i: (i, 0)),
        ],
        out_specs=pl.BlockSpec((tm, d), lambda i: (i, 0)),
        out_shape=jax.ShapeDtypeStruct((m, d), F32),
        compiler_params=_params(("arbitrary",)),
        name="out_proj",
    )(o, w, h)


def _ffn_kernel(hp_ref, h_ref, hn_ref, g_ref, wg_ref, wv_ref, cw_ref, cb_ref, wo_ref, out_ref,
                xn_ref, gate_ref, acc_ref, *, tm):
    i = pl.program_id(0)
    f = pl.program_id(1)

    @pl.when(f == 0)
    def _():
        gain = g_ref[...]
        prev = jnp.where(i > 0, _rms(hp_ref[...], gain), 0.0)
        nxt = jnp.where(i < pl.num_programs(0) - 1, _rms(hn_ref[...], gain), 0.0)
        xn_ref[0:HALO, :] = prev.astype(xn_ref.dtype)
        xn_ref[HALO:HALO + tm, :] = _rms(h_ref[...], gain).astype(xn_ref.dtype)
        xn_ref[HALO + tm:, :] = nxt.astype(xn_ref.dtype)
        acc_ref[...] = jnp.zeros_like(acc_ref)

    gate_ref[...] = _dot(xn_ref[...], wg_ref[...])
    val = _dot(xn_ref[HALO:HALO + tm, :], wv_ref[...])
    gc = cb_ref[...]
    for tap in range(CONV_W):
        lo = HALO - CONV_W // 2 + tap
        gc = gc + gate_ref[lo:lo + tm, :] * cw_ref[tap:tap + 1, :]
    a = gc * (1.0 / (1.0 + jnp.exp(-gc))) * val
    acc_ref[...] += _dot(a.astype(wo_ref.dtype), wo_ref[...])

    @pl.when(f == pl.num_programs(1) - 1)
    def _():
        out_ref[...] = h_ref[...] + acc_ref[...]


def _ffn(h, gain, w_in, conv_w, conv_b, w_out, layer):
    m, d = h.shape
    d_ff = w_out.shape[1]
    tm = _row_tile(m)
    tf = _col_tile(d_ff)
    n_i = m // tm
    hb = tm // HALO
    return pl.pallas_call(
        functools.partial(_ffn_kernel, tm=tm),
        grid=(n_i, d_ff // tf),
        in_specs=[
            pl.BlockSpec((HALO, d), lambda i, f: (jnp.maximum(i * hb - 1, 0), 0)),
            pl.BlockSpec((tm, d), lambda i, f: (i, 0)),
            pl.BlockSpec((HALO, d), lambda i, f: (jnp.minimum((i + 1) * hb, n_i * hb - 1), 0)),
            pl.BlockSpec((1, d), lambda i, f: (0, 0)),
            pl.BlockSpec((None, d, tf), lambda i, f: (layer, 0, f)),
            pl.BlockSpec((None, d, tf), lambda i, f: (layer, 0, d_ff // tf + f)),
            pl.BlockSpec((CONV_W, tf), lambda i, f: (0, f)),
            pl.BlockSpec((1, tf), lambda i, f: (0, f)),
            pl.BlockSpec((None, tf, d), lambda i, f: (layer, f, 0)),
        ],
        out_specs=pl.BlockSpec((tm, d), lambda i, f: (i, 0)),
        out_shape=jax.ShapeDtypeStruct((m, d), F32),
        scratch_shapes=[
            pltpu.VMEM((tm + 2 * HALO, d), COMPUTE_DTYPE),
            pltpu.VMEM((tm + 2 * HALO, tf), F32),
            pltpu.VMEM((tm, d), F32),
        ],
        compiler_params=_params(("arbitrary", "arbitrary")),
        name="conv_glu",
    )(h, h, h, gain.reshape(1, d), w_in, w_in, conv_w, conv_b.reshape(1, d_ff), w_out)


def _final_norm_kernel(h_ref, g_ref, o_ref):
    o_ref[...] = _rms(h_ref[...], g_ref[...])


def _final_norm(h, gain, batch, t):
    m, d = h.shape
    lseg = m // batch
    tt = _row_tile(t)
    return pl.pallas_call(
        _final_norm_kernel,
        grid=(batch, t // tt),
        in_specs=[
            pl.BlockSpec((pl.Element(tt), pl.Element(d)),
                         lambda b, i: (pl.multiple_of(b * lseg + META_BLOCK + i * tt, META_BLOCK), 0)),
            pl.BlockSpec((1, d), lambda b, i: (0, 0)),
        ],
        out_specs=pl.BlockSpec((None, tt, d), lambda b, i: (b, i, 0)),
        out_shape=jax.ShapeDtypeStruct((batch, t, d), F32),
        compiler_params=_params(("arbitrary", "arbitrary")),
        name="final_norm",
    )(h, gain.reshape(1, d))


def _rope_tables(batch, t):
    tok = jnp.arange(t)
    row = (tok // GRID_W).astype(F32)
    col = (tok % GRID_W).astype(F32)
    n_axis_pairs = HEAD_DIM // 4
    inv_freq = ROPE_THETA ** (-jnp.arange(n_axis_pairs, dtype=F32) / n_axis_pairs)
    ang = jnp.concatenate([row[:, None] * inv_freq[None], col[:, None] * inv_freq[None]], axis=-1)
    ang = jnp.concatenate([jnp.zeros((META_BLOCK, HEAD_DIM // 2), F32), ang], axis=0)
    cos, sin = jnp.cos(ang), jnp.sin(ang)
    cos = jnp.concatenate([cos, cos], axis=-1)
    sin = jnp.concatenate([-sin, sin], axis=-1)
    return jnp.tile(cos, (batch, 1)), jnp.tile(sin, (batch, 1))


def _trunk(x, meta_tokens, w):
    batch, t, d = x.shape
    assert t % GRID_W == 0 and d % HEAD_DIM == 0
    rows = t // GRID_W
    n_heads = d // HEAD_DIM
    scale = HEAD_DIM ** -0.5
    meta = jnp.broadcast_to(meta_tokens.astype(x.dtype)[None], (batch, N_META, d))
    h = jnp.concatenate([jnp.zeros((batch, META_OFF, d), x.dtype), meta, x], axis=1)
    h = h.reshape(batch * (META_BLOCK + t), d)
    depth = w["ffn_norm"].shape[0]
    rope = None
    for i in range(depth):
        j = i // 2
        if i % 2 == 0:
            qkv = _norm_proj(h, w["a_norm"][j], w["a_w_qkv"][j], n_heads, n_heads, scale)
            o = _na_attention(qkv, _na_bias_table(w["a_rpb"][j]), batch, rows, n_heads)
            h = _out_proj(o, w["a_w_o"][j], h)
        else:
            n_kv = (w["b_w_qkv"].shape[-1] // HEAD_DIM - n_heads) // 2
            if rope is None:
                rope = _rope_tables(batch, t)
            qkv = _norm_proj(h, w["b_norm"][j], w["b_w_qkv"][j], n_heads, n_kv, scale * LOG2E,
                             qk=(w["b_q_norm"][j], w["b_k_norm"][j]) + rope)
            o = _gqa_attention(qkv, batch, n_heads, n_kv)
            h = _out_proj(o, w["b_w_o"][j], h)
        h = _ffn(h, w["ffn_norm"][i], w["ffn_w_in"], w["ffn_conv_w"][i], w["ffn_conv_b"][i],
                 w["ffn_w_out"], i)
    return _final_norm(h, w["final_norm"], batch, t)


def kernel(x_prompt, x_sample, meta_tokens, a_norm, a_w_qkv, a_rpb, a_w_o, b_norm, b_w_qkv, b_q_norm,
           b_k_norm, b_w_o, ffn_norm, ffn_w_in, ffn_conv_w, ffn_conv_b, ffn_w_out, final_norm):
    cast = lambda a: a.astype(COMPUTE_DTYPE)
    w = dict(a_norm=a_norm, a_w_qkv=cast(a_w_qkv), a_rpb=a_rpb, a_w_o=cast(a_w_o), b_norm=b_norm,
             b_w_qkv=cast(b_w_qkv), b_q_norm=b_q_norm, b_k_norm=b_k_norm, b_w_o=cast(b_w_o),
             ffn_norm=ffn_norm, ffn_w_in=cast(ffn_w_in), ffn_conv_w=ffn_conv_w, ffn_conv_b=ffn_conv_b,
             ffn_w_out=cast(ffn_w_out), final_norm=final_norm)
    return (_trunk(x_prompt, meta_tokens, w), _trunk(x_sample, meta_tokens, w))
```

```python
import functools

import numpy as np
import jax
import jax.numpy as jnp
from jax import lax
from jax.experimental import pallas as pl
from jax.experimental.pallas import tpu as pltpu

HEAD_DIM = 128
N_META = 16
GRID_W = 64
NA_WIN_ROWS = 8
NA_WIN_COLS = 16
ROPE_THETA = 10000.0
EPS = 1e-6
CONV_W = 3

META_BLOCK = 128
META_OFF = META_BLOCK - N_META
NA_QROWS = 4
NA_KROWS = 12
HALO = 16
GQA_TK = 512
GQA_UNROLL = 10
GQA_BOUND_SLACK = 1.0 + 2.0 ** -6
GQA_BOUND_LIMIT = 40.0
NEG = -1e30
LOG2E = 1.4426950408889634
COMPUTE_DTYPE = jnp.bfloat16
V7X_VMEM_LIMIT_BYTES = 60 * 1024 * 1024

F32 = jnp.float32


def _dot(a, b):
    return jnp.dot(a, b, preferred_element_type=F32)


def _dot_nt(a, b):
    return lax.dot_general(a, b, (((1,), (1,)), ((), ())), preferred_element_type=F32)


def _rms(x, gain):
    ms = jnp.mean(x * x, axis=-1, keepdims=True)
    return x * lax.rsqrt(ms + EPS) * gain


def _params(semantics):
    return pltpu.CompilerParams(dimension_semantics=semantics, vmem_limit_bytes=V7X_VMEM_LIMIT_BYTES)


def _row_tile(m):
    for t in (768, 512, 384, 256, 128):
        if m % t == 0:
            return t
    raise ValueError(f"row count {m} is not a multiple of 128")


def _col_tile(n):
    for t in (512, 256, 128):
        if n % t == 0:
            return t
    raise ValueError(f"column count {n} is not a multiple of 128")


def _norm_proj_kernel(*refs, n_q_tiles, n_k_tiles, heads_per_tile, qk_norm, scale):
    if qk_norm:
        x_ref, g_ref, w_ref, qg_ref, kg_ref, cos_ref, sin_ref, o_ref, xn_ref = refs
    else:
        x_ref, g_ref, w_ref, o_ref, xn_ref = refs
    j = pl.program_id(1)

    @pl.when(j == 0)
    def _():
        xn_ref[...] = _rms(x_ref[...], g_ref[...]).astype(xn_ref.dtype)

    acc = _dot(xn_ref[...], w_ref[...])
    is_q = j < n_q_tiles
    tile_scale = jnp.where(is_q, scale, 1.0).astype(F32)

    def store(fn):
        for hh in range(heads_per_tile):
            o_ref[hh] = fn(acc[:, hh * HEAD_DIM:(hh + 1) * HEAD_DIM]).astype(o_ref.dtype)

    if not qk_norm:
        store(lambda y: y * tile_scale)
    else:
        is_qk = j < n_q_tiles + n_k_tiles
        gain = jnp.where(is_q, qg_ref[...], kg_ref[...])

        def rotary(y):
            yn = _rms(y, gain)
            return (yn * cos_ref[...] + pltpu.roll(yn, HEAD_DIM // 2, axis=1) * sin_ref[...]) * tile_scale

        @pl.when(is_qk)
        def _():
            store(rotary)

        @pl.when(jnp.logical_not(is_qk))
        def _():
            store(lambda y: y)


def _norm_proj(h, gain, w, n_q_heads, n_k_heads, scale, qk=None):
    m, d = h.shape
    n = w.shape[1]
    tm = _row_tile(m)
    tn = next(t for t in (8 * HEAD_DIM, 4 * HEAD_DIM, HEAD_DIM) if (n_k_heads * HEAD_DIM) % t == 0)
    hpt = tn // HEAD_DIM
    assert n % tn == 0 and (n_q_heads * HEAD_DIM) % tn == 0 and (n_k_heads * HEAD_DIM) % tn == 0
    kern = functools.partial(
        _norm_proj_kernel, n_q_tiles=n_q_heads // hpt, n_k_tiles=n_k_heads // hpt,
        heads_per_tile=hpt, qk_norm=qk is not None, scale=scale)
    in_specs = [
        pl.BlockSpec((tm, d), lambda i, j: (i, 0)),
        pl.BlockSpec((1, d), lambda i, j: (0, 0)),
        pl.BlockSpec((d, tn), lambda i, j: (0, j)),
    ]
    args = [h, gain.reshape(1, d), w]
    if qk is not None:
        q_gain, k_gain, cos, sin = qk
        in_specs += [
            pl.BlockSpec((1, HEAD_DIM), lambda i, j: (0, 0)),
            pl.BlockSpec((1, HEAD_DIM), lambda i, j: (0, 0)),
            pl.BlockSpec((tm, HEAD_DIM), lambda i, j: (i, 0)),
            pl.BlockSpec((tm, HEAD_DIM), lambda i, j: (i, 0)),
        ]
        args += [q_gain.reshape(1, HEAD_DIM), k_gain.reshape(1, HEAD_DIM), cos, sin]
    return pl.pallas_call(
        kern,
        grid=(m // tm, n // tn),
        in_specs=in_specs,
        out_specs=pl.BlockSpec((hpt, tm, HEAD_DIM), lambda i, j: (j, i, 0)),
        out_shape=jax.ShapeDtypeStruct((n // HEAD_DIM, m, HEAD_DIM), COMPUTE_DTYPE),
        scratch_shapes=[pltpu.VMEM((tm, d), COMPUTE_DTYPE)],
        compiler_params=_params(("arbitrary", "arbitrary")),
        name="norm_proj_rope" if qk is not None else "norm_proj",
    )(*args)


def _meta_key_bias():
    col = lax.broadcasted_iota(jnp.int32, (1, META_BLOCK), 1)
    return jnp.where(col >= META_OFF, 0.0, NEG).astype(F32)


def _na_kernel(q_ref, k_ref, v_ref, bias_ref, o_ref, vext_ref, s0_ref, s1_ref, p0_ref, p1_ref,
                k0_ref, k1_ref, v0_ref, v1_ref, *, rows):
    n_blocks = rows // NA_QROWS
    tq = NA_QROWS * GRID_W
    tk = NA_KROWS * GRID_W
    s_refs, p_refs = (s0_ref, s1_ref), (p0_ref, p1_ref)
    k_stage, v_stage = (k0_ref, k1_ref), (v0_ref, v1_ref)
    meta_bias = _meta_key_bias()

    vext_ref[:, 0:HEAD_DIM] = v_ref[...]
    vext_ref[:, HEAD_DIM:] = jnp.ones((vext_ref.shape[0], HEAD_DIM), vext_ref.dtype)

    k0 = k_ref[0:META_BLOCK, :]
    sm = _dot_nt(q_ref[0:META_BLOCK, :], k0) + meta_bias
    pm = jnp.exp(sm - jnp.max(sm, axis=-1, keepdims=True))
    om = _dot(pm.astype(vext_ref.dtype), vext_ref[0:META_BLOCK, :])
    row = lax.broadcasted_iota(jnp.int32, (META_BLOCK, 1), 0)
    o_ref[0:META_BLOCK, :] = jnp.where(
        row >= META_OFF, om[:, 0:HEAD_DIM] / om[:, HEAD_DIM:], 0.0).astype(o_ref.dtype)

    def query_start(i):
        return pl.multiple_of(META_BLOCK + i * tq, META_BLOCK)

    def window_start(i):
        w0 = jnp.clip(i * NA_QROWS - NA_WIN_ROWS // 2, 0, rows - NA_KROWS)
        return pl.multiple_of(META_BLOCK + w0 * GRID_W, META_BLOCK)

    def block_variant(i):
        return jnp.where(i == 0, 0, jnp.where(i == n_blocks - 1, 2, 1))

    def meta_slot(i):
        return pl.multiple_of(jnp.where(i == n_blocks - 1, 0, (NA_KROWS - 1) * GRID_W), N_META)

    def scores(i, buf):
        q = q_ref[pl.ds(query_start(i), tq), :]
        k_stage[buf][...] = k_ref[pl.ds(window_start(i), tk), :]
        k_stage[buf][pl.ds(meta_slot(i), N_META), :] = k_ref[META_OFF:META_BLOCK, :]
        s_refs[buf][...] = _dot_nt(q, k_stage[buf][...]) + bias_ref[block_variant(i)]

    def softmax(buf):
        n_tiles = tk // HEAD_DIM
        tiles = [s_refs[buf][:, t * HEAD_DIM:(t + 1) * HEAD_DIM] for t in range(n_tiles)]
        m = jnp.max(functools.reduce(jnp.maximum, tiles), axis=-1, keepdims=True)
        for t in range(n_tiles):
            p_refs[buf][:, t * HEAD_DIM:(t + 1) * HEAD_DIM] = jnp.exp(tiles[t] - m).astype(p_refs[buf].dtype)

    def output(i, buf):
        v_stage[buf][...] = vext_ref[pl.ds(window_start(i), tk), :]
        v_stage[buf][pl.ds(meta_slot(i), N_META), :] = vext_ref[META_OFF:META_BLOCK, :]
        o = _dot(p_refs[buf][...], v_stage[buf][...])
        o_ref[pl.ds(query_start(i), tq), :] = (o[:, 0:HEAD_DIM] / o[:, HEAD_DIM:]).astype(o_ref.dtype)

    scores(0, 0)
    scores(1, 1)
    softmax(0)

    def body(t, carry):
        i = 2 * t
        scores(i + 2, 0)
        softmax(1)
        output(i, 0)
        scores(i + 3, 1)
        softmax(0)
        output(i + 1, 1)
        return carry

    lax.fori_loop(0, n_blocks // 2 - 1, body, 0)
    softmax(1)
    output(n_blocks - 2, 0)
    output(n_blocks - 1, 1)


def _na_bias_table(rpb):
    qi = np.arange(NA_QROWS)[:, None]
    kj = np.arange(NA_KROWS)[None, :]
    half = NA_WIN_ROWS // 2
    row_lo = [np.zeros_like(qi), qi, np.full_like(qi, half)]
    row_off = [0, -half, -NA_WIN_ROWS]
    ridx, rvalid = [], []
    for lo, off in zip(row_lo, row_off):
        rvalid.append((kj >= lo) & (kj < lo + NA_WIN_ROWS))
        ridx.append(kj + off - qi + (NA_WIN_ROWS - 1))
    ridx = np.stack(ridx)
    rvalid = np.stack(rvalid)
    cols = np.arange(GRID_W)
    cstart = np.clip(cols - NA_WIN_COLS // 2, 0, GRID_W - NA_WIN_COLS)[:, None]
    kc = cols[None, :]
    cvalid = (kc >= cstart) & (kc < cstart + NA_WIN_COLS)
    cidx = kc - cols[:, None] + (NA_WIN_COLS - 1)
    valid = rvalid[:, :, None, :, None] & cvalid[None, None, :, None, :]
    n_r, n_c = rpb.shape[1], rpb.shape[2]
    rsel = (np.where(rvalid, ridx, 0)[..., None] == np.arange(n_r)).astype(np.float32)
    csel = (np.arange(n_c)[:, None, None] == np.where(cvalid, cidx, 0)[None]).astype(np.float32)
    hi = lax.Precision.HIGHEST
    by_row = jnp.einsum("hrc,vqjr->hvqjc", rpb.astype(F32), rsel, precision=hi)
    bias = jnp.einsum("hvqjc,cxy->hvqxjy", by_row, csel, precision=hi)
    bias = jnp.where(valid[None], bias, NEG)
    slot_row = np.array([NA_KROWS - 1, NA_KROWS - 1, 0])
    meta_cols = ((np.arange(NA_KROWS)[None, :, None] == slot_row[:, None, None])
                 & (np.arange(GRID_W)[None, None, :] < N_META))
    bias = jnp.where(meta_cols[None, :, None, None, :, :], 0.0, bias)
    return bias.reshape(rpb.shape[0], 3, NA_QROWS * GRID_W, NA_KROWS * GRID_W)


def _na_attention(qkv, bias, batch, seq_rows, n_heads):
    _, m, _ = qkv.shape
    lseg = m // batch
    rows = seq_rows
    assert rows % (2 * NA_QROWS) == 0 and rows >= NA_KROWS
    blk = (None, lseg, HEAD_DIM)
    tq, tk = NA_QROWS * GRID_W, NA_KROWS * GRID_W
    return pl.pallas_call(
        functools.partial(_na_kernel, rows=rows),
        grid=(batch, n_heads),
        in_specs=[
            pl.BlockSpec(blk, lambda b, h: (h, b, 0)),
            pl.BlockSpec(blk, lambda b, h: (n_heads + h, b, 0)),
            pl.BlockSpec(blk, lambda b, h: (2 * n_heads + h, b, 0)),
            pl.BlockSpec((None,) + bias.shape[1:], lambda b, h: (h, 0, 0, 0)),
        ],
        out_specs=pl.BlockSpec(blk, lambda b, h: (h, b, 0)),
        out_shape=jax.ShapeDtypeStruct((n_heads, m, HEAD_DIM), COMPUTE_DTYPE),
        scratch_shapes=[
            pltpu.VMEM((lseg, 2 * HEAD_DIM), COMPUTE_DTYPE),
            pltpu.VMEM((tq, tk), F32),
            pltpu.VMEM((tq, tk), F32),
            pltpu.VMEM((tq, tk), COMPUTE_DTYPE),
            pltpu.VMEM((tq, tk), COMPUTE_DTYPE),
            pltpu.VMEM((tk, HEAD_DIM), COMPUTE_DTYPE),
            pltpu.VMEM((tk, HEAD_DIM), COMPUTE_DTYPE),
            pltpu.VMEM((tk, 2 * HEAD_DIM), COMPUTE_DTYPE),
            pltpu.VMEM((tk, 2 * HEAD_DIM), COMPUTE_DTYPE),
        ],
        compiler_params=_params(("arbitrary", "arbitrary")),
        name="na_attention",
    )(qkv, qkv, qkv, bias)


def _gqa_kernel(q_ref, k_ref, v_ref, o_ref, vext_ref, kmax_ref, s0_ref, s1_ref, p0_ref, p1_ref,
                a0_ref, a1_ref, m_ref, sm_ref, acc_ref, *, group, tq, n_sub, n_chunks):
    qb = pl.program_id(2)
    n_tiles = GQA_TK // HEAD_DIM
    s_refs, p_refs, a_refs = (s0_ref, s1_ref), (p0_ref, p1_ref), (a0_ref, a1_ref)

    def queries(sub):
        return jnp.concatenate([q_ref[g, sub * tq:(sub + 1) * tq, :] for g in range(group)], axis=0)

    def chunk_start(c):
        return pl.multiple_of(META_BLOCK + c * GQA_TK, META_BLOCK)

    def scores(sub, c, buf):
        s_refs[buf][...] = _dot_nt(queries(sub), k_ref[pl.ds(chunk_start(c), GQA_TK), :])

    @pl.when(qb == 0)
    def _():
        vext_ref[:, 0:HEAD_DIM] = v_ref[...]
        vext_ref[:, HEAD_DIM:] = jnp.ones((vext_ref.shape[0], HEAD_DIM), vext_ref.dtype)

        def key_norm(c, best):
            kc = k_ref[pl.ds(chunk_start(c), GQA_TK), :].astype(F32)
            return jnp.maximum(best, jnp.sum(kc * kc, axis=-1, keepdims=True))

        k0 = k_ref[0:META_BLOCK, :].astype(F32)
        best = lax.fori_loop(0, n_chunks, key_norm, jnp.zeros((GQA_TK, 1), F32))
        kmax2 = jnp.maximum(jnp.max(best), jnp.max(jnp.sum(k0 * k0, axis=-1, keepdims=True)))
        kmax_ref[...] = jnp.full(kmax_ref.shape, jnp.sqrt(kmax2), F32)

    bounds = []
    for sub in range(n_sub):
        qf = queries(sub).astype(F32)
        bounds.append((jnp.sqrt(jnp.sum(qf * qf, axis=-1, keepdims=True)) * kmax_ref[0:1, 0:1])
                      * GQA_BOUND_SLACK)
    use_bound = functools.reduce(jnp.maximum, [jnp.max(b) for b in bounds]) <= GQA_BOUND_LIMIT

    def softmax(buf, fixed):
        for g in range(group):
            r0 = g * tq
            tiles = [s_refs[buf][r0:r0 + tq, t * HEAD_DIM:(t + 1) * HEAD_DIM] for t in range(n_tiles)]
            if fixed:
                m_new = m_ref[r0:r0 + tq, :]
            else:
                m_prev = m_ref[r0:r0 + tq, :]
                m_cur = functools.reduce(jnp.maximum, tiles)
                m_new = jnp.maximum(m_prev, jnp.max(m_cur, axis=-1, keepdims=True))
                m_ref[r0:r0 + tq, :] = m_new
                a_refs[buf][r0:r0 + tq, :] = jnp.exp2(m_prev - m_new)
            for t in range(n_tiles):
                p_refs[buf][r0:r0 + tq, t * HEAD_DIM:(t + 1) * HEAD_DIM] = (
                    jnp.exp2(tiles[t] - m_new).astype(p_refs[buf].dtype))

    def accumulate(c, buf, fixed):
        pv = _dot(p_refs[buf][...], vext_ref[pl.ds(chunk_start(c), GQA_TK), :])
        if fixed:
            acc_ref[...] = acc_ref[...] + pv
        else:
            alpha = a_refs[buf][...]
            acc_ref[...] = acc_ref[...] * jnp.concatenate([alpha, alpha], axis=1) + pv

    def attend(sub, fixed):
        bound = bounds[sub]
        scores(sub, 0, 0)
        scores(sub, 1, 1)
        if fixed:
            m_ref[...] = jnp.broadcast_to(bound, m_ref.shape)
        else:
            m_ref[...] = jnp.full(m_ref.shape, NEG, F32)
        acc_ref[...] = jnp.zeros_like(acc_ref)
        sm_ref[...] = _dot_nt(queries(sub), k_ref[0:META_BLOCK, :]) + _meta_key_bias()
        softmax(0, fixed)

        def steps(c, count):
            for d in range(count):
                scores(sub, c + d + 2, d % 2)
                softmax((d + 1) % 2, fixed)
                accumulate(c + d, d % 2, fixed)

        main = n_chunks - 2
        unroll = next(u for u in (GQA_UNROLL, 4, 2) if main % u == 0 or u == 2)
        trips = main // unroll

        def body(i, carry):
            steps(i * unroll, unroll)
            return carry

        lax.fori_loop(0, trips, body, 0)
        steps(trips * unroll, main - trips * unroll)
        softmax(1, fixed)
        m_prev = m_ref[...]
        s_meta = sm_ref[...]
        if fixed:
            m_new = m_prev
        else:
            m_new = jnp.maximum(m_prev, jnp.max(s_meta, axis=-1, keepdims=True))
            alpha = jnp.exp2(m_prev - m_new)
        pv_meta = _dot(jnp.exp2(s_meta - m_new).astype(vext_ref.dtype), vext_ref[0:META_BLOCK, :])
        accumulate(n_chunks - 2, 0, fixed)
        accumulate(n_chunks - 1, 1, fixed)
        if fixed:
            acc = acc_ref[...] + pv_meta
        else:
            acc = acc_ref[...] * jnp.concatenate([alpha, alpha], axis=1) + pv_meta
        o = acc[:, 0:HEAD_DIM] / acc[:, HEAD_DIM:]
        row = (qb * n_sub + sub) * tq + lax.broadcasted_iota(jnp.int32, (tq, 1), 0)
        keep = row >= META_OFF
        for g in range(group):
            o_ref[g, sub * tq:(sub + 1) * tq, :] = jnp.where(
                keep, o[g * tq:(g + 1) * tq], 0.0).astype(o_ref.dtype)

    @pl.when(use_bound)
    def _():
        for sub in range(n_sub):
            attend(sub, True)

    @pl.when(jnp.logical_not(use_bound))
    def _():
        for sub in range(n_sub):
            attend(sub, False)


def _gqa_attention(qkv, batch, n_q_heads, n_kv_heads):
    _, m, _ = qkv.shape
    lseg = m // batch
    group = n_q_heads // n_kv_heads
    tq = META_BLOCK
    n_sub = next(n for n in (3, 2, 1) if (lseg // tq) % n == 0)
    nqb = lseg // (tq * n_sub)
    assert (lseg - META_BLOCK) % (2 * GQA_TK) == 0
    n_chunks = (lseg - META_BLOCK) // GQA_TK
    kv_blk = (None, lseg, HEAD_DIM)
    q_blk = (group, tq * n_sub, HEAD_DIM)
    mq = group * tq
    return pl.pallas_call(
        functools.partial(_gqa_kernel, group=group, tq=tq, n_sub=n_sub, n_chunks=n_chunks),
        grid=(batch, n_kv_heads, nqb),
        in_specs=[
            pl.BlockSpec(q_blk, lambda b, kv, i: (kv, b * nqb + i, 0)),
            pl.BlockSpec(kv_blk, lambda b, kv, i: (n_q_heads + kv, b, 0)),
            pl.BlockSpec(kv_blk, lambda b, kv, i: (n_q_heads + n_kv_heads + kv, b, 0)),
        ],
        out_specs=pl.BlockSpec(q_blk, lambda b, kv, i: (kv, b * nqb + i, 0)),
        out_shape=jax.ShapeDtypeStruct((n_q_heads, m, HEAD_DIM), COMPUTE_DTYPE),
        scratch_shapes=[
            pltpu.VMEM((lseg, 2 * HEAD_DIM), COMPUTE_DTYPE),
            pltpu.VMEM((8, HEAD_DIM), F32),
            pltpu.VMEM((mq, GQA_TK), F32),
            pltpu.VMEM((mq, GQA_TK), F32),
            pltpu.VMEM((mq, GQA_TK), COMPUTE_DTYPE),
            pltpu.VMEM((mq, GQA_TK), COMPUTE_DTYPE),
            pltpu.VMEM((mq, HEAD_DIM), F32),
            pltpu.VMEM((mq, HEAD_DIM), F32),
            pltpu.VMEM((mq, HEAD_DIM), F32),
            pltpu.VMEM((mq, META_BLOCK), F32),
            pltpu.VMEM((mq, 2 * HEAD_DIM), F32),
        ],
        compiler_params=_params(("arbitrary", "arbitrary", "arbitrary")),
        name="gqa_attention",
    )(qkv, qkv, qkv)


def _out_proj_kernel(o_ref, w_ref, h_ref, out_ref, *, n_heads):
    o = jnp.concatenate([o_ref[hh] for hh in range(n_heads)], axis=1)
    out_ref[...] = h_ref[...] + _dot(o, w_ref[...])


def _out_proj(o, w, h):
    n_heads, m, _ = o.shape
    d = h.shape[1]
    tm = _row_tile(m) // 2 if _row_tile(m) >= 256 else _row_tile(m)
    return pl.pallas_call(
        functools.partial(_out_proj_kernel, n_heads=n_heads),
        grid=(m // tm,),
        in_specs=[
            pl.BlockSpec((n_heads, tm, HEAD_DIM), lambda i: (0, i, 0)),
            pl.BlockSpec(w.shape, lambda i: (0, 0)),
            pl.BlockSpec((tm, d), lambda i: (i, 0)),
        ],
        out_specs=pl.BlockSpec((tm, d), lambda i: (i, 0)),
        out_shape=jax.ShapeDtypeStruct((m, d), F32),
        compiler_params=_params(("arbitrary",)),
        name="out_proj",
    )(o, w, h)


def _ffn_kernel(hp_ref, h_ref, hn_ref, g_ref, wg_ref, wv_ref, cw_ref, cb_ref, wo_ref, out_ref,
                xn_ref, gate_ref, acc_ref, *, tm):
    i = pl.program_id(0)
    f = pl.program_id(1)

    @pl.when(f == 0)
    def _():
        gain = g_ref[...]
        prev = jnp.where(i > 0, _rms(hp_ref[...], gain), 0.0)
        nxt = jnp.where(i < pl.num_programs(0) - 1, _rms(hn_ref[...], gain), 0.0)
        xn_ref[0:HALO, :] = prev.astype(xn_ref.dtype)
        xn_ref[HALO:HALO + tm, :] = _rms(h_ref[...], gain).astype(xn_ref.dtype)
        xn_ref[HALO + tm:, :] = nxt.astype(xn_ref.dtype)
        acc_ref[...] = jnp.zeros_like(acc_ref)

    gate_ref[...] = _dot(xn_ref[...], wg_ref[...])
    val = _dot(xn_ref[HALO:HALO + tm, :], wv_ref[...])
    gc = cb_ref[...]
    for tap in range(CONV_W):
        lo = HALO - CONV_W // 2 + tap
        gc = gc + gate_ref[lo:lo + tm, :] * cw_ref[tap:tap + 1, :]
    a = gc * (1.0 / (1.0 + jnp.exp(-gc))) * val
    acc_ref[...] += _dot(a.astype(wo_ref.dtype), wo_ref[...])

    @pl.when(f == pl.num_programs(1) - 1)
    def _():
        out_ref[...] = h_ref[...] + acc_ref[...]


def _ffn(h, gain, w_in, conv_w, conv_b, w_out, layer):
    m, d = h.shape
    d_ff = w_out.shape[1]
    tm = _row_tile(m)
    tf = _col_tile(d_ff)
    n_i = m // tm
    hb = tm // HALO
    return pl.pallas_call(
        functools.partial(_ffn_kernel, tm=tm),
        grid=(n_i, d_ff // tf),
        in_specs=[
            pl.BlockSpec((HALO, d), lambda i, f: (jnp.maximum(i * hb - 1, 0), 0)),
            pl.BlockSpec((tm, d), lambda i, f: (i, 0)),
            pl.BlockSpec((HALO, d), lambda i, f: (jnp.minimum((i + 1) * hb, n_i * hb - 1), 0)),
            pl.BlockSpec((1, d), lambda i, f: (0, 0)),
            pl.BlockSpec((None, d, tf), lambda i, f: (layer, 0, f)),
            pl.BlockSpec((None, d, tf), lambda i, f: (layer, 0, d_ff // tf + f)),
            pl.BlockSpec((CONV_W, tf), lambda i, f: (0, f)),
            pl.BlockSpec((1, tf), lambda i, f: (0, f)),
            pl.BlockSpec((None, tf, d), lambda i, f: (layer, f, 0)),
        ],
        out_specs=pl.BlockSpec((tm, d), lambda i, f: (i, 0)),
        out_shape=jax.ShapeDtypeStruct((m, d), F32),
        scratch_shapes=[
            pltpu.VMEM((tm + 2 * HALO, d), COMPUTE_DTYPE),
            pltpu.VMEM((tm + 2 * HALO, tf), F32),
            pltpu.VMEM((tm, d), F32),
        ],
        compiler_params=_params(("arbitrary", "arbitrary")),
        name="conv_glu",
    )(h, h, h, gain.reshape(1, d), w_in, w_in, conv_w, conv_b.reshape(1, d_ff), w_out)


def _final_norm_kernel(h_ref, g_ref, o_ref):
    o_ref[...] = _rms(h_ref[...], g_ref[...])


def _final_norm(h, gain, batch, t):
    m, d = h.shape
    lseg = m // batch
    tt = _row_tile(t)
    return pl.pallas_call(
        _final_norm_kernel,
        grid=(batch, t // tt),
        in_specs=[
            pl.BlockSpec((pl.Element(tt), pl.Element(d)),
                         lambda b, i: (pl.multiple_of(b * lseg + META_BLOCK + i * tt, META_BLOCK), 0)),
            pl.BlockSpec((1, d), lambda b, i: (0, 0)),
        ],
        out_specs=pl.BlockSpec((None, tt, d), lambda b, i: (b, i, 0)),
        out_shape=jax.ShapeDtypeStruct((batch, t, d), F32),
        compiler_params=_params(("arbitrary", "arbitrary")),
        name="final_norm",
    )(h, gain.reshape(1, d))


def _rope_tables(batch, t):
    tok = jnp.arange(t)
    row = (tok // GRID_W).astype(F32)
    col = (tok % GRID_W).astype(F32)
    n_axis_pairs = HEAD_DIM // 4
    inv_freq = ROPE_THETA ** (-jnp.arange(n_axis_pairs, dtype=F32) / n_axis_pairs)
    ang = jnp.concatenate([row[:, None] * inv_freq[None], col[:, None] * inv_freq[None]], axis=-1)
    ang = jnp.concatenate([jnp.zeros((META_BLOCK, HEAD_DIM // 2), F32), ang], axis=0)
    cos, sin = jnp.cos(ang), jnp.sin(ang)
    cos = jnp.concatenate([cos, cos], axis=-1)
    sin = jnp.concatenate([-sin, sin], axis=-1)
    return jnp.tile(cos, (batch, 1)), jnp.tile(sin, (batch, 1))


def _trunk(x, meta_tokens, w):
    batch, t, d = x.shape
    assert t % GRID_W == 0 and d % HEAD_DIM == 0
    rows = t // GRID_W
    n_heads = d // HEAD_DIM
    scale = HEAD_DIM ** -0.5
    meta = jnp.broadcast_to(meta_tokens.astype(x.dtype)[None], (batch, N_META, d))
    h = jnp.concatenate([jnp.zeros((batch, META_OFF, d), x.dtype), meta, x], axis=1)
    h = h.reshape(batch * (META_BLOCK + t), d)
    depth = w["ffn_norm"].shape[0]
    rope = None
    for i in range(depth):
        j = i // 2
        if i % 2 == 0:
            qkv = _norm_proj(h, w["a_norm"][j], w["a_w_qkv"][j], n_heads, n_heads, scale)
            o = _na_attention(qkv, _na_bias_table(w["a_rpb"][j]), batch, rows, n_heads)
            h = _out_proj(o, w["a_w_o"][j], h)
        else:
            n_kv = (w["b_w_qkv"].shape[-1] // HEAD_DIM - n_heads) // 2
            if rope is None:
                rope = _rope_tables(batch, t)
            qkv = _norm_proj(h, w["b_norm"][j], w["b_w_qkv"][j], n_heads, n_kv, scale * LOG2E,
                             qk=(w["b_q_norm"][j], w["b_k_norm"][j]) + rope)
            o = _gqa_attention(qkv, batch, n_heads, n_kv)
            h = _out_proj(o, w["b_w_o"][j], h)
        h = _ffn(h, w["ffn_norm"][i], w["ffn_w_in"], w["ffn_conv_w"][i], w["ffn_conv_b"][i],
                 w["ffn_w_out"], i)
    return _final_norm(h, w["final_norm"], batch, t)


def kernel(x_prompt, x_sample, meta_tokens, a_norm, a_w_qkv, a_rpb, a_w_o, b_norm, b_w_qkv, b_q_norm,
           b_k_norm, b_w_o, ffn_norm, ffn_w_in, ffn_conv_w, ffn_conv_b, ffn_w_out, final_norm):
    cast = lambda a: a.astype(COMPUTE_DTYPE)
    w = dict(a_norm=a_norm, a_w_qkv=cast(a_w_qkv), a_rpb=a_rpb, a_w_o=cast(a_w_o), b_norm=b_norm,
             b_w_qkv=cast(b_w_qkv), b_q_norm=b_q_norm, b_k_norm=b_k_norm, b_w_o=cast(b_w_o),
             ffn_norm=ffn_norm, ffn_w_in=cast(ffn_w_in), ffn_conv_w=ffn_conv_w, ffn_conv_b=ffn_conv_b,
             ffn_w_out=cast(ffn_w_out), final_norm=final_norm)
    return (_trunk(x_prompt, meta_tokens, w), _trunk(x_sample, meta_tokens, w))
```

```python
import functools

import numpy as np
import jax
import jax.numpy as jnp
from jax import lax
from jax.experimental import pallas as pl
from jax.experimental.pallas import tpu as pltpu

HEAD_DIM = 128
N_META = 16
GRID_W = 64
NA_WIN_ROWS = 8
NA_WIN_COLS = 16
ROPE_THETA = 10000.0
EPS = 1e-6
CONV_W = 3

META_BLOCK = 128
META_OFF = META_BLOCK - N_META
NA_QROWS = 4
NA_KROWS = 12
HALO = 16
GQA_TK = 512
GQA_UNROLL = 10
GQA_BOUND_SLACK = 1.0 + 2.0 ** -6
GQA_BOUND_LIMIT = 40.0
NEG = -1e30
LOG2E = 1.4426950408889634
COMPUTE_DTYPE = jnp.bfloat16
V7X_VMEM_LIMIT_BYTES = 60 * 1024 * 1024

F32 = jnp.float32


def _dot(a, b):
    return jnp.dot(a, b, preferred_element_type=F32)


def _dot_nt(a, b):
    return lax.dot_general(a, b, (((1,), (1,)), ((), ())), preferred_element_type=F32)


def _rms(x, gain):
    ms = jnp.mean(x * x, axis=-1, keepdims=True)
    return x * lax.rsqrt(ms + EPS) * gain


def _params(semantics):
    return pltpu.CompilerParams(dimension_semantics=semantics, vmem_limit_bytes=V7X_VMEM_LIMIT_BYTES)


def _row_tile(m):
    for t in (768, 512, 384, 256, 128):
        if m % t == 0:
            return t
    raise ValueError(f"row count {m} is not a multiple of 128")


def _col_tile(n):
    for t in (512, 256, 128):
        if n % t == 0:
            return t
    raise ValueError(f"column count {n} is not a multiple of 128")


def _norm_proj_kernel(*refs, n_q_tiles, n_k_tiles, heads_per_tile, qk_norm, scale):
    if qk_norm:
        x_ref, g_ref, w_ref, qg_ref, kg_ref, cos_ref, sin_ref, o_ref, xn_ref = refs
    else:
        x_ref, g_ref, w_ref, o_ref, xn_ref = refs
    j = pl.program_id(1)

    @pl.when(j == 0)
    def _():
        xn_ref[...] = _rms(x_ref[...], g_ref[...]).astype(xn_ref.dtype)

    acc = _dot(xn_ref[...], w_ref[...])
    is_q = j < n_q_tiles
    tile_scale = jnp.where(is_q, scale, 1.0).astype(F32)

    def store(fn):
        for hh in range(heads_per_tile):
            o_ref[hh] = fn(acc[:, hh * HEAD_DIM:(hh + 1) * HEAD_DIM]).astype(o_ref.dtype)

    if not qk_norm:
        store(lambda y: y * tile_scale)
    else:
        is_qk = j < n_q_tiles + n_k_tiles
        gain = jnp.where(is_q, qg_ref[...], kg_ref[...])

        def rotary(y):
            yn = _rms(y, gain)
            return (yn * cos_ref[...] + pltpu.roll(yn, HEAD_DIM // 2, axis=1) * sin_ref[...]) * tile_scale

        @pl.when(is_qk)
        def _():
            store(rotary)

        @pl.when(jnp.logical_not(is_qk))
        def _():
            store(lambda y: y)


def _norm_proj(h, gain, w, n_q_heads, n_k_heads, scale, qk=None):
    m, d = h.shape
    n = w.shape[1]
    tm = _row_tile(m)
    tn = next(t for t in (8 * HEAD_DIM, 4 * HEAD_DIM, HEAD_DIM) if (n_k_heads * HEAD_DIM) % t == 0)
    hpt = tn // HEAD_DIM
    assert n % tn == 0 and (n_q_heads * HEAD_DIM) % tn == 0 and (n_k_heads * HEAD_DIM) % tn == 0
    kern = functools.partial(
        _norm_proj_kernel, n_q_tiles=n_q_heads // hpt, n_k_tiles=n_k_heads // hpt,
        heads_per_tile=hpt, qk_norm=qk is not None, scale=scale)
    in_specs = [
        pl.BlockSpec((tm, d), lambda i, j: (i, 0)),
        pl.BlockSpec((1, d), lambda i, j: (0, 0)),
        pl.BlockSpec((d, tn), lambda i, j: (0, j)),
    ]
    args = [h, gain.reshape(1, d), w]
    if qk is not None:
        q_gain, k_gain, cos, sin = qk
        in_specs += [
            pl.BlockSpec((1, HEAD_DIM), lambda i, j: (0, 0)),
            pl.BlockSpec((1, HEAD_DIM), lambda i, j: (0, 0)),
            pl.BlockSpec((tm, HEAD_DIM), lambda i, j: (i, 0)),
            pl.BlockSpec((tm, HEAD_DIM), lambda i, j: (i, 0)),
        ]
        args += [q_gain.reshape(1, HEAD_DIM), k_gain.reshape(1, HEAD_DIM), cos, sin]
    return pl.pallas_call(
        kern,
        grid=(m // tm, n // tn),
        in_specs=in_specs,
        out_specs=pl.BlockSpec((hpt, tm, HEAD_DIM), lambda i, j: (j, i, 0)),
        out_shape=jax.ShapeDtypeStruct((n // HEAD_DIM, m, HEAD_DIM), COMPUTE_DTYPE),
        scratch_shapes=[pltpu.VMEM((tm, d), COMPUTE_DTYPE)],
        compiler_params=_params(("arbitrary", "arbitrary")),
        name="norm_proj_rope" if qk is not None else "norm_proj",
    )(*args)


def _meta_key_bias():
    col = lax.broadcasted_iota(jnp.int32, (1, META_BLOCK), 1)
    return jnp.where(col >= META_OFF, 0.0, NEG).astype(F32)


def _na_kernel(q_ref, k_ref, v_ref, bias_ref, o_ref, vext_ref, s0_ref, s1_ref, p0_ref, p1_ref,
                k0_ref, k1_ref, v0_ref, v1_ref, *, rows):
    n_blocks = rows // NA_QROWS
    tq = NA_QROWS * GRID_W
    tk = NA_KROWS * GRID_W
    s_refs, p_refs = (s0_ref, s1_ref), (p0_ref, p1_ref)
    k_stage, v_stage = (k0_ref, k1_ref), (v0_ref, v1_ref)
    meta_bias = _meta_key_bias()

    vext_ref[:, 0:HEAD_DIM] = v_ref[...]
    vext_ref[:, HEAD_DIM:] = jnp.ones((vext_ref.shape[0], HEAD_DIM), vext_ref.dtype)

    k0 = k_ref[0:META_BLOCK, :]
    sm = _dot_nt(q_ref[0:META_BLOCK, :], k0) + meta_bias
    pm = jnp.exp(sm - jnp.max(sm, axis=-1, keepdims=True))
    om = _dot(pm.astype(vext_ref.dtype), vext_ref[0:META_BLOCK, :])
    row = lax.broadcasted_iota(jnp.int32, (META_BLOCK, 1), 0)
    o_ref[0:META_BLOCK, :] = jnp.where(
        row >= META_OFF, om[:, 0:HEAD_DIM] / om[:, HEAD_DIM:], 0.0).astype(o_ref.dtype)

    def query_start(i):
        return pl.multiple_of(META_BLOCK + i * tq, META_BLOCK)

    def window_start(i):
        w0 = jnp.clip(i * NA_QROWS - NA_WIN_ROWS // 2, 0, rows - NA_KROWS)
        return pl.multiple_of(META_BLOCK + w0 * GRID_W, META_BLOCK)

    def block_variant(i):
        return jnp.where(i == 0, 0, jnp.where(i == n_blocks - 1, 2, 1))

    def meta_slot(i):
        return pl.multiple_of(jnp.where(i == n_blocks - 1, 0, (NA_KROWS - 1) * GRID_W), N_META)

    def scores(i, buf):
        q = q_ref[pl.ds(query_start(i), tq), :]
        k_stage[buf][...] = k_ref[pl.ds(window_start(i), tk), :]
        k_stage[buf][pl.ds(meta_slot(i), N_META), :] = k_ref[META_OFF:META_BLOCK, :]
        s_refs[buf][...] = _dot_nt(q, k_stage[buf][...]) + bias_ref[block_variant(i)]

    def softmax(buf):
        n_tiles = tk // HEAD_DIM
        tiles = [s_refs[buf][:, t * HEAD_DIM:(t + 1) * HEAD_DIM] for t in range(n_tiles)]
        m = jnp.max(functools.reduce(jnp.maximum, tiles), axis=-1, keepdims=True)
        for t in range(n_tiles):
            p_refs[buf][:, t * HEAD_DIM:(t + 1) * HEAD_DIM] = jnp.exp(tiles[t] - m).astype(p_refs[buf].dtype)

    def output(i, buf):
        v_stage[buf][...] = vext_ref[pl.ds(window_start(i), tk), :]
        v_stage[buf][pl.ds(meta_slot(i), N_META), :] = vext_ref[META_OFF:META_BLOCK, :]
        o = _dot(p_refs[buf][...], v_stage[buf][...])
        o_ref[pl.ds(query_start(i), tq), :] = (o[:, 0:HEAD_DIM] / o[:, HEAD_DIM:]).astype(o_ref.dtype)

    scores(0, 0)
    scores(1, 1)
    softmax(0)

    def body(t, carry):
        i = 2 * t
        scores(i + 2, 0)
        softmax(1)
        output(i, 0)
        scores(i + 3, 1)
        softmax(0)
        output(i + 1, 1)
        return carry

    lax.fori_loop(0, n_blocks // 2 - 1, body, 0)
    softmax(1)
    output(n_blocks - 2, 0)
    output(n_blocks - 1, 1)


def _na_bias_table(rpb):
    qi = np.arange(NA_QROWS)[:, None]
    kj = np.arange(NA_KROWS)[None, :]
    half = NA_WIN_ROWS // 2
    row_lo = [np.zeros_like(qi), qi, np.full_like(qi, half)]
    row_off = [0, -half, -NA_WIN_ROWS]
    ridx, rvalid = [], []
    for lo, off in zip(row_lo, row_off):
        rvalid.append((kj >= lo) & (kj < lo + NA_WIN_ROWS))
        ridx.append(kj + off - qi + (NA_WIN_ROWS - 1))
    ridx = np.stack(ridx)
    rvalid = np.stack(rvalid)
    cols = np.arange(GRID_W)
    cstart = np.clip(cols - NA_WIN_COLS // 2, 0, GRID_W - NA_WIN_COLS)[:, None]
    kc = cols[None, :]
    cvalid = (kc >= cstart) & (kc < cstart + NA_WIN_COLS)
    cidx = kc - cols[:, None] + (NA_WIN_COLS - 1)
    valid = rvalid[:, :, None, :, None] & cvalid[None, None, :, None, :]
    n_r, n_c = rpb.shape[1], rpb.shape[2]
    rsel = (np.where(rvalid, ridx, 0)[..., None] == np.arange(n_r)).astype(np.float32)
    csel = (np.arange(n_c)[:, None, None] == np.where(cvalid, cidx, 0)[None]).astype(np.float32)
    hi = lax.Precision.HIGHEST
    by_row = jnp.einsum("hrc,vqjr->hvqjc", rpb.astype(F32), rsel, precision=hi)
    bias = jnp.einsum("hvqjc,cxy->hvqxjy", by_row, csel, precision=hi)
    bias = jnp.where(valid[None], bias, NEG)
    slot_row = np.array([NA_KROWS - 1, NA_KROWS - 1, 0])
    meta_cols = ((np.arange(NA_KROWS)[None, :, None] == slot_row[:, None, None])
                 & (np.arange(GRID_W)[None, None, :] < N_META))
    bias = jnp.where(meta_cols[None, :, None, None, :, :], 0.0, bias)
    return bias.reshape(rpb.shape[0], 3, NA_QROWS * GRID_W, NA_KROWS * GRID_W)


def _na_attention(qkv, bias, batch, seq_rows, n_heads):
    _, m, _ = qkv.shape
    lseg = m // batch
    rows = seq_rows
    assert rows % (2 * NA_QROWS) == 0 and rows >= NA_KROWS
    blk = (None, lseg, HEAD_DIM)
    tq, tk = NA_QROWS * GRID_W, NA_KROWS * GRID_W
    return pl.pallas_call(
        functools.partial(_na_kernel, rows=rows),
        grid=(batch, n_heads),
        in_specs=[
            pl.BlockSpec(blk, lambda b, h: (h, b, 0)),
            pl.BlockSpec(blk, lambda b, h: (n_heads + h, b, 0)),
            pl.BlockSpec(blk, lambda b, h: (2 * n_heads + h, b, 0)),
            pl.BlockSpec((None,) + bias.shape[1:], lambda b, h: (h, 0, 0, 0)),
        ],
        out_specs=pl.BlockSpec(blk, lambda b, h: (h, b, 0)),
        out_shape=jax.ShapeDtypeStruct((n_heads, m, HEAD_DIM), COMPUTE_DTYPE),
        scratch_shapes=[
            pltpu.VMEM((lseg, 2 * HEAD_DIM), COMPUTE_DTYPE),
            pltpu.VMEM((tq, tk), F32),
            pltpu.VMEM((tq, tk), F32),
            pltpu.VMEM((tq, tk), COMPUTE_DTYPE),
            pltpu.VMEM((tq, tk), COMPUTE_DTYPE),
            pltpu.VMEM((tk, HEAD_DIM), COMPUTE_DTYPE),
            pltpu.VMEM((tk, HEAD_DIM), COMPUTE_DTYPE),
            pltpu.VMEM((tk, 2 * HEAD_DIM), COMPUTE_DTYPE),
            pltpu.VMEM((tk, 2 * HEAD_DIM), COMPUTE_DTYPE),
        ],
        compiler_params=_params(("arbitrary", "arbitrary")),
        name="na_attention",
    )(qkv, qkv, qkv, bias)


def _gqa_kernel(q_ref, k_ref, v_ref, o_ref, vext_ref, kmax_ref, s0_ref, s1_ref, p0_ref, p1_ref,
                a0_ref, a1_ref, m_ref, sm_ref, acc_ref, *, group, tq, n_sub, n_chunks):
    qb = pl.program_id(2)
    n_tiles = GQA_TK // HEAD_DIM
    s_refs, p_refs, a_refs = (s0_ref, s1_ref), (p0_ref, p1_ref), (a0_ref, a1_ref)

    def queries(sub):
        return jnp.concatenate([q_ref[g, sub * tq:(sub + 1) * tq, :] for g in range(group)], axis=0)

    def chunk_start(c):
        return pl.multiple_of(META_BLOCK + c * GQA_TK, META_BLOCK)

    def scores(sub, c, buf):
        s_refs[buf][...] = _dot_nt(queries(sub), k_ref[pl.ds(chunk_start(c), GQA_TK), :])

    @pl.when(qb == 0)
    def _():
        vext_ref[:, 0:HEAD_DIM] = v_ref[...]
        vext_ref[:, HEAD_DIM:] = jnp.ones((vext_ref.shape[0], HEAD_DIM), vext_ref.dtype)

        def key_norm(c, best):
            kc = k_ref[pl.ds(chunk_start(c), GQA_TK), :].astype(F32)
            return jnp.maximum(best, jnp.sum(kc * kc, axis=-1, keepdims=True))

        k0 = k_ref[0:META_BLOCK, :].astype(F32)
        best = lax.fori_loop(0, n_chunks, key_norm, jnp.zeros((GQA_TK, 1), F32))
        kmax2 = jnp.maximum(jnp.max(best), jnp.max(jnp.sum(k0 * k0, axis=-1, keepdims=True)))
        kmax_ref[...] = jnp.full(kmax_ref.shape, jnp.sqrt(kmax2), F32)

    bounds = []
    for sub in range(n_sub):
        qf = queries(sub).astype(F32)
        bounds.append((jnp.sqrt(jnp.sum(qf * qf, axis=-1, keepdims=True)) * kmax_ref[0:1, 0:1])
                      * GQA_BOUND_SLACK)
    use_bound = functools.reduce(jnp.maximum, [jnp.max(b) for b in bounds]) <= GQA_BOUND_LIMIT

    def softmax(buf, fixed):
        for g in range(group):
            r0 = g * tq
            tiles = [s_refs[buf][r0:r0 + tq, t * HEAD_DIM:(t + 1) * HEAD_DIM] for t in range(n_tiles)]
            if fixed:
                m_new = m_ref[r0:r0 + tq, :]
            else:
                m_prev = m_ref[r0:r0 + tq, :]
                m_cur = functools.reduce(jnp.maximum, tiles)
                m_new = jnp.maximum(m_prev, jnp.max(m_cur, axis=-1, keepdims=True))
                m_ref[r0:r0 + tq, :] = m_new
                a_refs[buf][r0:r0 + tq, :] = jnp.exp2(m_prev - m_new)
            for t in range(n_tiles):
                p_refs[buf][r0:r0 + tq, t * HEAD_DIM:(t + 1) * HEAD_DIM] = (
                    jnp.exp2(tiles[t] - m_new).astype(p_refs[buf].dtype))

    def accumulate(c, buf, fixed):
        pv = _dot(p_refs[buf][...], vext_ref[pl.ds(chunk_start(c), GQA_TK), :])
        if fixed:
            acc_ref[...] = acc_ref[...] + pv
        else:
            alpha = a_refs[buf][...]
            acc_ref[...] = acc_ref[...] * jnp.concatenate([alpha, alpha], axis=1) + pv

    def attend(sub, fixed):
        bound = bounds[sub]
        scores(sub, 0, 0)
        scores(sub, 1, 1)
        if fixed:
            m_ref[...] = jnp.broadcast_to(bound, m_ref.shape)
        else:
            m_ref[...] = jnp.full(m_ref.shape, NEG, F32)
        acc_ref[...] = jnp.zeros_like(acc_ref)
        sm_ref[...] = _dot_nt(queries(sub), k_ref[0:META_BLOCK, :]) + _meta_key_bias()
        softmax(0, fixed)

        def steps(c, count):
            for d in range(count):
                scores(sub, c + d + 2, d % 2)
                softmax((d + 1) % 2, fixed)
                accumulate(c + d, d % 2, fixed)

        main = n_chunks - 2
        unroll = next(u for u in (GQA_UNROLL, 4, 2) if u % 2 == 0 and (main % u == 0 or u == 2))
        trips = main // unroll

        def body(i, carry):
            steps(i * unroll, unroll)
            return carry

        lax.fori_loop(0, trips, body, 0)
        steps(trips * unroll, main - trips * unroll)
        softmax(1, fixed)
        m_prev = m_ref[...]
        s_meta = sm_ref[...]
        if fixed:
            m_new = m_prev
        else:
            m_new = jnp.maximum(m_prev, jnp.max(s_meta, axis=-1, keepdims=True))
            alpha = jnp.exp2(m_prev - m_new)
        pv_meta = _dot(jnp.exp2(s_meta - m_new).astype(vext_ref.dtype), vext_ref[0:META_BLOCK, :])
        accumulate(n_chunks - 2, 0, fixed)
        accumulate(n_chunks - 1, 1, fixed)
        if fixed:
            acc = acc_ref[...] + pv_meta
        else:
            acc = acc_ref[...] * jnp.concatenate([alpha, alpha], axis=1) + pv_meta
        o = acc[:, 0:HEAD_DIM] / acc[:, HEAD_DIM:]
        row = (qb * n_sub + sub) * tq + lax.broadcasted_iota(jnp.int32, (tq, 1), 0)
        keep = row >= META_OFF
        for g in range(group):
            o_ref[g, sub * tq:(sub + 1) * tq, :] = jnp.where(
                keep, o[g * tq:(g + 1) * tq], 0.0).astype(o_ref.dtype)

    @pl.when(use_bound)
    def _():
        for sub in range(n_sub):
            attend(sub, True)

    @pl.when(jnp.logical_not(use_bound))
    def _():
        for sub in range(n_sub):
            attend(sub, False)


def _gqa_attention(qkv, batch, n_q_heads, n_kv_heads):
    _, m, _ = qkv.shape
    lseg = m // batch
    group = n_q_heads // n_kv_heads
    tq = META_BLOCK
    n_sub = next(n for n in (3, 2, 1) if (lseg // tq) % n == 0)
    nqb = lseg // (tq * n_sub)
    assert (lseg - META_BLOCK) % (2 * GQA_TK) == 0
    n_chunks = (lseg - META_BLOCK) // GQA_TK
    kv_blk = (None, lseg, HEAD_DIM)
    q_blk = (group, tq * n_sub, HEAD_DIM)
    mq = group * tq
    return pl.pallas_call(
        functools.partial(_gqa_kernel, group=group, tq=tq, n_sub=n_sub, n_chunks=n_chunks),
        grid=(batch, n_kv_heads, nqb),
        in_specs=[
            pl.BlockSpec(q_blk, lambda b, kv, i: (kv, b * nqb + i, 0)),
            pl.BlockSpec(kv_blk, lambda b, kv, i: (n_q_heads + kv, b, 0)),
            pl.BlockSpec(kv_blk, lambda b, kv, i: (n_q_heads + n_kv_heads + kv, b, 0)),
        ],
        out_specs=pl.BlockSpec(q_blk, lambda b, kv, i: (kv, b * nqb + i, 0)),
        out_shape=jax.ShapeDtypeStruct((n_q_heads, m, HEAD_DIM), COMPUTE_DTYPE),
        scratch_shapes=[
            pltpu.VMEM((lseg, 2 * HEAD_DIM), COMPUTE_DTYPE),
            pltpu.VMEM((8, HEAD_DIM), F32),
            pltpu.VMEM((mq, GQA_TK), F32),
            pltpu.VMEM((mq, GQA_TK), F32),
            pltpu.VMEM((mq, GQA_TK), COMPUTE_DTYPE),
            pltpu.VMEM((mq, GQA_TK), COMPUTE_DTYPE),
            pltpu.VMEM((mq, HEAD_DIM), F32),
            pltpu.VMEM((mq, HEAD_DIM), F32),
            pltpu.VMEM((mq, HEAD_DIM), F32),
            pltpu.VMEM((mq, META_BLOCK), F32),
            pltpu.VMEM((mq, 2 * HEAD_DIM), F32),
        ],
        compiler_params=_params(("arbitrary", "arbitrary", "arbitrary")),
        name="gqa_attention",
    )(qkv, qkv, qkv)


def _out_proj_kernel(o_ref, w_ref, h_ref, out_ref, *, n_heads):
    o = jnp.concatenate([o_ref[hh] for hh in range(n_heads)], axis=1)
    out_ref[...] = h_ref[...] + _dot(o, w_ref[...])


def _out_proj(o, w, h):
    n_heads, m, _ = o.shape
    d = h.shape[1]
    tm = _row_tile(m) // 2 if _row_tile(m) >= 256 else _row_tile(m)
    return pl.pallas_call(
        functools.partial(_out_proj_kernel, n_heads=n_heads),
        grid=(m // tm,),
        in_specs=[
            pl.BlockSpec((n_heads, tm, HEAD_DIM), lambda i: (0, i, 0)),
            pl.BlockSpec(w.shape, lambda i: (0, 0)),
            pl.BlockSpec((tm, d), lambda i: (i, 0)),
        ],
        out_specs=pl.BlockSpec((tm, d), lambda i: (i, 0)),
        out_shape=jax.ShapeDtypeStruct((m, d), F32),
        compiler_params=_params(("arbitrary",)),
        name="out_proj",
    )(o, w, h)


def _ffn_kernel(hp_ref, h_ref, hn_ref, g_ref, wg_ref, wv_ref, cw_ref, cb_ref, wo_ref, out_ref,
                xn_ref, gate_ref, acc_ref, *, tm):
    i = pl.program_id(0)
    f = pl.program_id(1)

    @pl.when(f == 0)
    def _():
        gain = g_ref[...]
        prev = jnp.where(i > 0, _rms(hp_ref[...], gain), 0.0)
        nxt = jnp.where(i < pl.num_programs(0) - 1, _rms(hn_ref[...], gain), 0.0)
        xn_ref[0:HALO, :] = prev.astype(xn_ref.dtype)
        xn_ref[HALO:HALO + tm, :] = _rms(h_ref[...], gain).astype(xn_ref.dtype)
        xn_ref[HALO + tm:, :] = nxt.astype(xn_ref.dtype)
        acc_ref[...] = jnp.zeros_like(acc_ref)

    gate_ref[...] = _dot(xn_ref[...], wg_ref[...])
    val = _dot(xn_ref[HALO:HALO + tm, :], wv_ref[...])
    gc = cb_ref[...]
    for tap in range(CONV_W):
        lo = HALO - CONV_W // 2 + tap
        gc = gc + gate_ref[lo:lo + tm, :] * cw_ref[tap:tap + 1, :]
    a = gc * (1.0 / (1.0 + jnp.exp(-gc))) * val
    acc_ref[...] += _dot(a.astype(wo_ref.dtype), wo_ref[...])

    @pl.when(f == pl.num_programs(1) - 1)
    def _():
        out_ref[...] = h_ref[...] + acc_ref[...]


def _ffn(h, gain, w_in, conv_w, conv_b, w_out, layer):
    m, d = h.shape
    d_ff = w_out.shape[1]
    tm = _row_tile(m)
    tf = _col_tile(d_ff)
    n_i = m // tm
    hb = tm // HALO
    return pl.pallas_call(
        functools.partial(_ffn_kernel, tm=tm),
        grid=(n_i, d_ff // tf),
        in_specs=[
            pl.BlockSpec((HALO, d), lambda i, f: (jnp.maximum(i * hb - 1, 0), 0)),
            pl.BlockSpec((tm, d), lambda i, f: (i, 0)),
            pl.BlockSpec((HALO, d), lambda i, f: (jnp.minimum((i + 1) * hb, n_i * hb - 1), 0)),
            pl.BlockSpec((1, d), lambda i, f: (0, 0)),
            pl.BlockSpec((None, d, tf), lambda i, f: (layer, 0, f)),
            pl.BlockSpec((None, d, tf), lambda i, f: (layer, 0, d_ff // tf + f)),
            pl.BlockSpec((CONV_W, tf), lambda i, f: (0, f)),
            pl.BlockSpec((1, tf), lambda i, f: (0, f)),
            pl.BlockSpec((None, tf, d), lambda i, f: (layer, f, 0)),
        ],
        out_specs=pl.BlockSpec((tm, d), lambda i, f: (i, 0)),
        out_shape=jax.ShapeDtypeStruct((m, d), F32),
        scratch_shapes=[
            pltpu.VMEM((tm + 2 * HALO, d), COMPUTE_DTYPE),
            pltpu.VMEM((tm + 2 * HALO, tf), F32),
            pltpu.VMEM((tm, d), F32),
        ],
        compiler_params=_params(("arbitrary", "arbitrary")),
        name="conv_glu",
    )(h, h, h, gain.reshape(1, d), w_in, w_in, conv_w, conv_b.reshape(1, d_ff), w_out)


def _final_norm_kernel(h_ref, g_ref, o_ref):
    o_ref[...] = _rms(h_ref[...], g_ref[...])


def _final_norm(h, gain, batch, t):
    m, d = h.shape
    lseg = m // batch
    tt = _row_tile(t)
    return pl.pallas_call(
        _final_norm_kernel,
        grid=(batch, t // tt),
        in_specs=[
            pl.BlockSpec((pl.Element(tt), pl.Element(d)),
                         lambda b, i: (pl.multiple_of(b * lseg + META_BLOCK + i * tt, META_BLOCK), 0)),
            pl.BlockSpec((1, d), lambda b, i: (0, 0)),
        ],
        out_specs=pl.BlockSpec((None, tt, d), lambda b, i: (b, i, 0)),
        out_shape=jax.ShapeDtypeStruct((batch, t, d), F32),
        compiler_params=_params(("arbitrary", "arbitrary")),
        name="final_norm",
    )(h, gain.reshape(1, d))


def _rope_tables(batch, t):
    tok = jnp.arange(t)
    row = (tok // GRID_W).astype(F32)
    col = (tok % GRID_W).astype(F32)
    n_axis_pairs = HEAD_DIM // 4
    inv_freq = ROPE_THETA ** (-jnp.arange(n_axis_pairs, dtype=F32) / n_axis_pairs)
    ang = jnp.concatenate([row[:, None] * inv_freq[None], col[:, None] * inv_freq[None]], axis=-1)
    ang = jnp.concatenate([jnp.zeros((META_BLOCK, HEAD_DIM // 2), F32), ang], axis=0)
    cos, sin = jnp.cos(ang), jnp.sin(ang)
    cos = jnp.concatenate([cos, cos], axis=-1)
    sin = jnp.concatenate([-sin, sin], axis=-1)
    return jnp.tile(cos, (batch, 1)), jnp.tile(sin, (batch, 1))


def _trunk(x, meta_tokens, w):
    batch, t, d = x.shape
    assert t % GRID_W == 0 and d % HEAD_DIM == 0
    rows = t // GRID_W
    n_heads = d // HEAD_DIM
    scale = HEAD_DIM ** -0.5
    meta = jnp.broadcast_to(meta_tokens.astype(x.dtype)[None], (batch, N_META, d))
    h = jnp.concatenate([jnp.zeros((batch, META_OFF, d), x.dtype), meta, x], axis=1)
    h = h.reshape(batch * (META_BLOCK + t), d)
    depth = w["ffn_norm"].shape[0]
    rope = None
    for i in range(depth):
        j = i // 2
        if i % 2 == 0:
            qkv = _norm_proj(h, w["a_norm"][j], w["a_w_qkv"][j], n_heads, n_heads, scale)
            o = _na_attention(qkv, _na_bias_table(w["a_rpb"][j]), batch, rows, n_heads)
            h = _out_proj(o, w["a_w_o"][j], h)
        else:
            n_kv = (w["b_w_qkv"].shape[-1] // HEAD_DIM - n_heads) // 2
            if rope is None:
                rope = _rope_tables(batch, t)
            qkv = _norm_proj(h, w["b_norm"][j], w["b_w_qkv"][j], n_heads, n_kv, scale * LOG2E,
                             qk=(w["b_q_norm"][j], w["b_k_norm"][j]) + rope)
            o = _gqa_attention(qkv, batch, n_heads, n_kv)
            h = _out_proj(o, w["b_w_o"][j], h)
        h = _ffn(h, w["ffn_norm"][i], w["ffn_w_in"], w["ffn_conv_w"][i], w["ffn_conv_b"][i],
                 w["ffn_w_out"], i)
    return _final_norm(h, w["final_norm"], batch, t)


def kernel(x_prompt, x_sample, meta_tokens, a_norm, a_w_qkv, a_rpb, a_w_o, b_norm, b_w_qkv, b_q_norm,
           b_k_norm, b_w_o, ffn_norm, ffn_w_in, ffn_conv_w, ffn_conv_b, ffn_w_out, final_norm):
    cast = lambda a: a.astype(COMPUTE_DTYPE)
    w = dict(a_norm=a_norm, a_w_qkv=cast(a_w_qkv), a_rpb=a_rpb, a_w_o=cast(a_w_o), b_norm=b_norm,
             b_w_qkv=cast(b_w_qkv), b_q_norm=b_q_norm, b_k_norm=b_k_norm, b_w_o=cast(b_w_o),
             ffn_norm=ffn_norm, ffn_w_in=cast(ffn_w_in), ffn_conv_w=ffn_conv_w, ffn_conv_b=ffn_conv_b,
             ffn_w_out=cast(ffn_w_out), final_norm=final_norm)
    return (_trunk(x_prompt, meta_tokens, w), _trunk(x_sample, meta_tokens, w))
```

```python
import functools

import numpy as np
import jax
import jax.numpy as jnp
from jax import lax
from jax.experimental import pallas as pl
from jax.experimental.pallas import tpu as pltpu

HEAD_DIM = 128
N_META = 16
GRID_W = 64
NA_WIN_ROWS = 8
NA_WIN_COLS = 16
ROPE_THETA = 10000.0
EPS = 1e-6
CONV_W = 3

META_BLOCK = 128
META_OFF = META_BLOCK - N_META
NA_QROWS = 4
NA_KROWS = 12
HALO = 16
GQA_TK = 512
GQA_UNROLL = 10
GQA_BOUND_SLACK = 1.0 + 2.0 ** -6
GQA_BOUND_LIMIT = 40.0
NEG = -1e30
LOG2E = 1.4426950408889634
COMPUTE_DTYPE = jnp.bfloat16
V7X_VMEM_LIMIT_BYTES = 60 * 1024 * 1024

F32 = jnp.float32


def _dot(a, b):
    return jnp.dot(a, b, preferred_element_type=F32)


def _dot_nt(a, b):
    return lax.dot_general(a, b, (((1,), (1,)), ((), ())), preferred_element_type=F32)


def _rms(x, gain):
    ms = jnp.mean(x * x, axis=-1, keepdims=True)
    return x * lax.rsqrt(ms + EPS) * gain


def _params(semantics):
    return pltpu.CompilerParams(dimension_semantics=semantics, vmem_limit_bytes=V7X_VMEM_LIMIT_BYTES)


def _row_tile(m):
    for t in (768, 512, 384, 256, 128):
        if m % t == 0:
            return t
    raise ValueError(f"row count {m} is not a multiple of 128")


def _col_tile(n):
    for t in (512, 256, 128):
        if n % t == 0:
            return t
    raise ValueError(f"column count {n} is not a multiple of 128")


def _norm_proj_kernel(*refs, n_q_tiles, n_k_tiles, heads_per_tile, qk_norm, scale):
    if qk_norm:
        x_ref, g_ref, w_ref, qg_ref, kg_ref, cos_ref, sin_ref, o_ref, xn_ref = refs
    else:
        x_ref, g_ref, w_ref, o_ref, xn_ref = refs
    j = pl.program_id(1)

    @pl.when(j == 0)
    def _():
        xn_ref[...] = _rms(x_ref[...], g_ref[...]).astype(xn_ref.dtype)

    is_q = j < n_q_tiles
    tile_scale = jnp.where(is_q, scale, 1.0).astype(F32)
    tm = xn_ref.shape[0]
    half = tm // 2 if (qk_norm and tm % 32 == 0) else tm

    def store(fn):
        for r0 in range(0, tm, half):
            acc = _dot(xn_ref[r0:r0 + half, :], w_ref[...])
            for hh in range(heads_per_tile):
                o_ref[hh, r0:r0 + half, :] = fn(
                    acc[:, hh * HEAD_DIM:(hh + 1) * HEAD_DIM], r0).astype(o_ref.dtype)

    if not qk_norm:
        store(lambda y, r0: y * tile_scale)
    else:
        is_qk = j < n_q_tiles + n_k_tiles
        gain = jnp.where(is_q, qg_ref[...], kg_ref[...])

        def rotary(y, r0):
            yn = _rms(y, gain)
            return (yn * cos_ref[r0:r0 + half, :]
                    + pltpu.roll(yn, HEAD_DIM // 2, axis=1) * sin_ref[r0:r0 + half, :]) * tile_scale

        @pl.when(is_qk)
        def _():
            store(rotary)

        @pl.when(jnp.logical_not(is_qk))
        def _():
            store(lambda y, r0: y)


def _norm_proj(h, gain, w, n_q_heads, n_k_heads, scale, qk=None):
    m, d = h.shape
    n = w.shape[1]
    tm = _row_tile(m)
    tn = next(t for t in (8 * HEAD_DIM, 4 * HEAD_DIM, HEAD_DIM) if (n_k_heads * HEAD_DIM) % t == 0)
    hpt = tn // HEAD_DIM
    assert n % tn == 0 and (n_q_heads * HEAD_DIM) % tn == 0 and (n_k_heads * HEAD_DIM) % tn == 0
    kern = functools.partial(
        _norm_proj_kernel, n_q_tiles=n_q_heads // hpt, n_k_tiles=n_k_heads // hpt,
        heads_per_tile=hpt, qk_norm=qk is not None, scale=scale)
    in_specs = [
        pl.BlockSpec((tm, d), lambda i, j: (i, 0)),
        pl.BlockSpec((1, d), lambda i, j: (0, 0)),
        pl.BlockSpec((d, tn), lambda i, j: (0, j)),
    ]
    args = [h, gain.reshape(1, d), w]
    if qk is not None:
        q_gain, k_gain, cos, sin = qk
        in_specs += [
            pl.BlockSpec((1, HEAD_DIM), lambda i, j: (0, 0)),
            pl.BlockSpec((1, HEAD_DIM), lambda i, j: (0, 0)),
            pl.BlockSpec((tm, HEAD_DIM), lambda i, j: (i, 0)),
            pl.BlockSpec((tm, HEAD_DIM), lambda i, j: (i, 0)),
        ]
        args += [q_gain.reshape(1, HEAD_DIM), k_gain.reshape(1, HEAD_DIM), cos, sin]
    return pl.pallas_call(
        kern,
        grid=(m // tm, n // tn),
        in_specs=in_specs,
        out_specs=pl.BlockSpec((hpt, tm, HEAD_DIM), lambda i, j: (j, i, 0)),
        out_shape=jax.ShapeDtypeStruct((n // HEAD_DIM, m, HEAD_DIM), COMPUTE_DTYPE),
        scratch_shapes=[pltpu.VMEM((tm, d), COMPUTE_DTYPE)],
        compiler_params=_params(("arbitrary", "arbitrary")),
        name="norm_proj_rope" if qk is not None else "norm_proj",
    )(*args)


def _meta_key_bias():
    col = lax.broadcasted_iota(jnp.int32, (1, META_BLOCK), 1)
    return jnp.where(col >= META_OFF, 0.0, NEG).astype(F32)


def _na_kernel(q_ref, k_ref, v_ref, bias_ref, o_ref, vext_ref, s0_ref, s1_ref, p0_ref, p1_ref,
                k0_ref, k1_ref, v0_ref, v1_ref, *, rows):
    n_blocks = rows // NA_QROWS
    tq = NA_QROWS * GRID_W
    tk = NA_KROWS * GRID_W
    s_refs, p_refs = (s0_ref, s1_ref), (p0_ref, p1_ref)
    k_stage, v_stage = (k0_ref, k1_ref), (v0_ref, v1_ref)
    meta_bias = _meta_key_bias()

    vext_ref[:, 0:HEAD_DIM] = v_ref[...]
    vext_ref[:, HEAD_DIM:] = jnp.ones((vext_ref.shape[0], HEAD_DIM), vext_ref.dtype)

    k0 = k_ref[0:META_BLOCK, :]
    sm = _dot_nt(q_ref[0:META_BLOCK, :], k0) + meta_bias
    pm = jnp.exp(sm - jnp.max(sm, axis=-1, keepdims=True))
    om = _dot(pm.astype(vext_ref.dtype), vext_ref[0:META_BLOCK, :])
    row = lax.broadcasted_iota(jnp.int32, (META_BLOCK, 1), 0)
    o_ref[0:META_BLOCK, :] = jnp.where(
        row >= META_OFF, om[:, 0:HEAD_DIM] / om[:, HEAD_DIM:], 0.0).astype(o_ref.dtype)

    def query_start(i):
        return pl.multiple_of(META_BLOCK + i * tq, META_BLOCK)

    def window_start(i):
        w0 = jnp.clip(i * NA_QROWS - NA_WIN_ROWS // 2, 0, rows - NA_KROWS)
        return pl.multiple_of(META_BLOCK + w0 * GRID_W, META_BLOCK)

    def block_variant(i):
        return jnp.where(i == 0, 0, jnp.where(i == n_blocks - 1, 2, 1))

    def meta_slot(i):
        return pl.multiple_of(jnp.where(i == n_blocks - 1, 0, (NA_KROWS - 1) * GRID_W), N_META)

    def scores(i, buf):
        q = q_ref[pl.ds(query_start(i), tq), :]
        k_stage[buf][...] = k_ref[pl.ds(window_start(i), tk), :]
        k_stage[buf][pl.ds(meta_slot(i), N_META), :] = k_ref[META_OFF:META_BLOCK, :]
        s_refs[buf][...] = _dot_nt(q, k_stage[buf][...]) + bias_ref[block_variant(i)]

    def softmax(buf):
        n_tiles = tk // HEAD_DIM
        tiles = [s_refs[buf][:, t * HEAD_DIM:(t + 1) * HEAD_DIM] for t in range(n_tiles)]
        m = jnp.max(functools.reduce(jnp.maximum, tiles), axis=-1, keepdims=True)
        for t in range(n_tiles):
            p_refs[buf][:, t * HEAD_DIM:(t + 1) * HEAD_DIM] = jnp.exp(tiles[t] - m).astype(p_refs[buf].dtype)

    def output(i, buf):
        v_stage[buf][...] = vext_ref[pl.ds(window_start(i), tk), :]
        v_stage[buf][pl.ds(meta_slot(i), N_META), :] = vext_ref[META_OFF:META_BLOCK, :]
        o = _dot(p_refs[buf][...], v_stage[buf][...])
        o_ref[pl.ds(query_start(i), tq), :] = (o[:, 0:HEAD_DIM] / o[:, HEAD_DIM:]).astype(o_ref.dtype)

    scores(0, 0)
    scores(1, 1)
    softmax(0)

    def body(t, carry):
        i = 2 * t
        scores(i + 2, 0)
        softmax(1)
        output(i, 0)
        scores(i + 3, 1)
        softmax(0)
        output(i + 1, 1)
        return carry

    lax.fori_loop(0, n_blocks // 2 - 1, body, 0)
    softmax(1)
    output(n_blocks - 2, 0)
    output(n_blocks - 1, 1)


def _na_bias_table(rpb):
    qi = np.arange(NA_QROWS)[:, None]
    kj = np.arange(NA_KROWS)[None, :]
    half = NA_WIN_ROWS // 2
    row_lo = [np.zeros_like(qi), qi, np.full_like(qi, half)]
    row_off = [0, -half, -NA_WIN_ROWS]
    ridx, rvalid = [], []
    for lo, off in zip(row_lo, row_off):
        rvalid.append((kj >= lo) & (kj < lo + NA_WIN_ROWS))
        ridx.append(kj + off - qi + (NA_WIN_ROWS - 1))
    ridx = np.stack(ridx)
    rvalid = np.stack(rvalid)
    cols = np.arange(GRID_W)
    cstart = np.clip(cols - NA_WIN_COLS // 2, 0, GRID_W - NA_WIN_COLS)[:, None]
    kc = cols[None, :]
    cvalid = (kc >= cstart) & (kc < cstart + NA_WIN_COLS)
    cidx = kc - cols[:, None] + (NA_WIN_COLS - 1)
    valid = rvalid[:, :, None, :, None] & cvalid[None, None, :, None, :]
    n_r, n_c = rpb.shape[1], rpb.shape[2]
    rsel = (np.where(rvalid, ridx, 0)[..., None] == np.arange(n_r)).astype(np.float32)
    csel = (np.arange(n_c)[:, None, None] == np.where(cvalid, cidx, 0)[None]).astype(np.float32)
    hi = lax.Precision.HIGHEST
    by_row = jnp.einsum("hrc,vqjr->hvqjc", rpb.astype(F32), rsel, precision=hi)
    bias = jnp.einsum("hvqjc,cxy->hvqxjy", by_row, csel, precision=hi)
    bias = jnp.where(valid[None], bias, NEG)
    slot_row = np.array([NA_KROWS - 1, NA_KROWS - 1, 0])
    meta_cols = ((np.arange(NA_KROWS)[None, :, None] == slot_row[:, None, None])
                 & (np.arange(GRID_W)[None, None, :] < N_META))
    bias = jnp.where(meta_cols[None, :, None, None, :, :], 0.0, bias)
    return bias.reshape(rpb.shape[0], 3, NA_QROWS * GRID_W, NA_KROWS * GRID_W)


def _na_attention(qkv, bias, batch, seq_rows, n_heads):
    _, m, _ = qkv.shape
    lseg = m // batch
    rows = seq_rows
    assert rows % (2 * NA_QROWS) == 0 and rows >= NA_KROWS
    blk = (None, lseg, HEAD_DIM)
    tq, tk = NA_QROWS * GRID_W, NA_KROWS * GRID_W
    return pl.pallas_call(
        functools.partial(_na_kernel, rows=rows),
        grid=(batch, n_heads),
        in_specs=[
            pl.BlockSpec(blk, lambda b, h: (h, b, 0)),
            pl.BlockSpec(blk, lambda b, h: (n_heads + h, b, 0)),
            pl.BlockSpec(blk, lambda b, h: (2 * n_heads + h, b, 0)),
            pl.BlockSpec((None,) + bias.shape[1:], lambda b, h: (h, 0, 0, 0)),
        ],
        out_specs=pl.BlockSpec(blk, lambda b, h: (h, b, 0)),
        out_shape=jax.ShapeDtypeStruct((n_heads, m, HEAD_DIM), COMPUTE_DTYPE),
        scratch_shapes=[
            pltpu.VMEM((lseg, 2 * HEAD_DIM), COMPUTE_DTYPE),
            pltpu.VMEM((tq, tk), F32),
            pltpu.VMEM((tq, tk), F32),
            pltpu.VMEM((tq, tk), COMPUTE_DTYPE),
            pltpu.VMEM((tq, tk), COMPUTE_DTYPE),
            pltpu.VMEM((tk, HEAD_DIM), COMPUTE_DTYPE),
            pltpu.VMEM((tk, HEAD_DIM), COMPUTE_DTYPE),
            pltpu.VMEM((tk, 2 * HEAD_DIM), COMPUTE_DTYPE),
            pltpu.VMEM((tk, 2 * HEAD_DIM), COMPUTE_DTYPE),
        ],
        compiler_params=_params(("arbitrary", "arbitrary")),
        name="na_attention",
    )(qkv, qkv, qkv, bias)


def _gqa_kernel(q_ref, k_ref, v_ref, o_ref, vext_ref, kmax_ref, s0_ref, s1_ref, p0_ref, p1_ref,
                a0_ref, a1_ref, m_ref, sm_ref, acc_ref, *, group, tq, n_sub, n_chunks):
    qb = pl.program_id(2)
    n_tiles = GQA_TK // HEAD_DIM
    s_refs, p_refs, a_refs = (s0_ref, s1_ref), (p0_ref, p1_ref), (a0_ref, a1_ref)

    def queries(sub):
        return jnp.concatenate([q_ref[g, sub * tq:(sub + 1) * tq, :] for g in range(group)], axis=0)

    def chunk_start(c):
        return pl.multiple_of(META_BLOCK + c * GQA_TK, META_BLOCK)

    def scores(sub, c, buf):
        s_refs[buf][...] = _dot_nt(queries(sub), k_ref[pl.ds(chunk_start(c), GQA_TK), :])

    @pl.when(qb == 0)
    def _():
        vext_ref[:, 0:HEAD_DIM] = v_ref[...]
        vext_ref[:, HEAD_DIM:] = jnp.ones((vext_ref.shape[0], HEAD_DIM), vext_ref.dtype)

        def key_norm(c, best):
            kc = k_ref[pl.ds(chunk_start(c), GQA_TK), :].astype(F32)
            return jnp.maximum(best, jnp.sum(kc * kc, axis=-1, keepdims=True))

        k0 = k_ref[0:META_BLOCK, :].astype(F32)
        best = lax.fori_loop(0, n_chunks, key_norm, jnp.zeros((GQA_TK, 1), F32))
        kmax2 = jnp.maximum(jnp.max(best), jnp.max(jnp.sum(k0 * k0, axis=-1, keepdims=True)))
        kmax_ref[...] = jnp.full(kmax_ref.shape, jnp.sqrt(kmax2), F32)

    bounds = []
    for sub in range(n_sub):
        qf = queries(sub).astype(F32)
        bounds.append((jnp.sqrt(jnp.sum(qf * qf, axis=-1, keepdims=True)) * kmax_ref[0:1, 0:1])
                      * GQA_BOUND_SLACK)
    use_bound = functools.reduce(jnp.maximum, [jnp.max(b) for b in bounds]) <= GQA_BOUND_LIMIT

    def softmax(buf, fixed):
        for g in range(group):
            r0 = g * tq
            tiles = [s_refs[buf][r0:r0 + tq, t * HEAD_DIM:(t + 1) * HEAD_DIM] for t in range(n_tiles)]
            if fixed:
                m_new = m_ref[r0:r0 + tq, :]
            else:
                m_prev = m_ref[r0:r0 + tq, :]
                m_cur = functools.reduce(jnp.maximum, tiles)
                m_new = jnp.maximum(m_prev, jnp.max(m_cur, axis=-1, keepdims=True))
                m_ref[r0:r0 + tq, :] = m_new
                a_refs[buf][r0:r0 + tq, :] = jnp.exp2(m_prev - m_new)
            for t in range(n_tiles):
                p_refs[buf][r0:r0 + tq, t * HEAD_DIM:(t + 1) * HEAD_DIM] = (
                    jnp.exp2(tiles[t] - m_new).astype(p_refs[buf].dtype))

    def accumulate(c, buf, fixed):
        pv = _dot(p_refs[buf][...], vext_ref[pl.ds(chunk_start(c), GQA_TK), :])
        if fixed:
            acc_ref[...] = acc_ref[...] + pv
        else:
            alpha = a_refs[buf][...]
            acc_ref[...] = acc_ref[...] * jnp.concatenate([alpha, alpha], axis=1) + pv

    def attend(sub, fixed):
        bound = bounds[sub]
        scores(sub, 0, 0)
        scores(sub, 1, 1)
        if fixed:
            m_ref[...] = jnp.broadcast_to(bound, m_ref.shape)
        else:
            m_ref[...] = jnp.full(m_ref.shape, NEG, F32)
        acc_ref[...] = jnp.zeros_like(acc_ref)
        sm_ref[...] = _dot_nt(queries(sub), k_ref[0:META_BLOCK, :]) + _meta_key_bias()
        softmax(0, fixed)

        def steps(c, count):
            for d in range(count):
                scores(sub, c + d + 2, d % 2)
                softmax((d + 1) % 2, fixed)
                accumulate(c + d, d % 2, fixed)

        main = n_chunks - 2
        unroll = next(u for u in (GQA_UNROLL, 4, 2) if u % 2 == 0 and (main % u == 0 or u == 2))
        trips = main // unroll

        def body(i, carry):
            steps(i * unroll, unroll)
            return carry

        lax.fori_loop(0, trips, body, 0)
        steps(trips * unroll, main - trips * unroll)
        softmax(1, fixed)
        m_prev = m_ref[...]
        s_meta = sm_ref[...]
        if fixed:
            m_new = m_prev
        else:
            m_new = jnp.maximum(m_prev, jnp.max(s_meta, axis=-1, keepdims=True))
            alpha = jnp.exp2(m_prev - m_new)
        pv_meta = _dot(jnp.exp2(s_meta - m_new).astype(vext_ref.dtype), vext_ref[0:META_BLOCK, :])
        accumulate(n_chunks - 2, 0, fixed)
        accumulate(n_chunks - 1, 1, fixed)
        if fixed:
            acc = acc_ref[...] + pv_meta
        else:
            acc = acc_ref[...] * jnp.concatenate([alpha, alpha], axis=1) + pv_meta
        o = acc[:, 0:HEAD_DIM] / acc[:, HEAD_DIM:]
        row = (qb * n_sub + sub) * tq + lax.broadcasted_iota(jnp.int32, (tq, 1), 0)
        keep = row >= META_OFF
        for g in range(group):
            o_ref[g, sub * tq:(sub + 1) * tq, :] = jnp.where(
                keep, o[g * tq:(g + 1) * tq], 0.0).astype(o_ref.dtype)

    @pl.when(use_bound)
    def _():
        for sub in range(n_sub):
            attend(sub, True)

    @pl.when(jnp.logical_not(use_bound))
    def _():
        for sub in range(n_sub):
            attend(sub, False)


def _gqa_attention(qkv, batch, n_q_heads, n_kv_heads):
    _, m, _ = qkv.shape
    lseg = m // batch
    group = n_q_heads // n_kv_heads
    tq = META_BLOCK
    n_sub = next(n for n in (3, 2, 1) if (lseg // tq) % n == 0)
    nqb = lseg // (tq * n_sub)
    assert (lseg - META_BLOCK) % (2 * GQA_TK) == 0
    n_chunks = (lseg - META_BLOCK) // GQA_TK
    kv_blk = (None, lseg, HEAD_DIM)
    q_blk = (group, tq * n_sub, HEAD_DIM)
    mq = group * tq
    return pl.pallas_call(
        functools.partial(_gqa_kernel, group=group, tq=tq, n_sub=n_sub, n_chunks=n_chunks),
        grid=(batch, n_kv_heads, nqb),
        in_specs=[
            pl.BlockSpec(q_blk, lambda b, kv, i: (kv, b * nqb + i, 0)),
            pl.BlockSpec(kv_blk, lambda b, kv, i: (n_q_heads + kv, b, 0)),
            pl.BlockSpec(kv_blk, lambda b, kv, i: (n_q_heads + n_kv_heads + kv, b, 0)),
        ],
        out_specs=pl.BlockSpec(q_blk, lambda b, kv, i: (kv, b * nqb + i, 0)),
        out_shape=jax.ShapeDtypeStruct((n_q_heads, m, HEAD_DIM), COMPUTE_DTYPE),
        scratch_shapes=[
            pltpu.VMEM((lseg, 2 * HEAD_DIM), COMPUTE_DTYPE),
            pltpu.VMEM((8, HEAD_DIM), F32),
            pltpu.VMEM((mq, GQA_TK), F32),
            pltpu.VMEM((mq, GQA_TK), F32),
            pltpu.VMEM((mq, GQA_TK), COMPUTE_DTYPE),
            pltpu.VMEM((mq, GQA_TK), COMPUTE_DTYPE),
            pltpu.VMEM((mq, HEAD_DIM), F32),
            pltpu.VMEM((mq, HEAD_DIM), F32),
            pltpu.VMEM((mq, HEAD_DIM), F32),
            pltpu.VMEM((mq, META_BLOCK), F32),
            pltpu.VMEM((mq, 2 * HEAD_DIM), F32),
        ],
        compiler_params=_params(("arbitrary", "arbitrary", "arbitrary")),
        name="gqa_attention",
    )(qkv, qkv, qkv)


def _out_proj_kernel(o_ref, w_ref, h_ref, out_ref, *, n_heads):
    o = jnp.concatenate([o_ref[hh] for hh in range(n_heads)], axis=1)
    out_ref[...] = h_ref[...] + _dot(o, w_ref[...])


def _out_proj(o, w, h):
    n_heads, m, _ = o.shape
    d = h.shape[1]
    tm = _row_tile(m) // 2 if _row_tile(m) >= 256 else _row_tile(m)
    return pl.pallas_call(
        functools.partial(_out_proj_kernel, n_heads=n_heads),
        grid=(m // tm,),
        in_specs=[
            pl.BlockSpec((n_heads, tm, HEAD_DIM), lambda i: (0, i, 0)),
            pl.BlockSpec(w.shape, lambda i: (0, 0)),
            pl.BlockSpec((tm, d), lambda i: (i, 0)),
        ],
        out_specs=pl.BlockSpec((tm, d), lambda i: (i, 0)),
        out_shape=jax.ShapeDtypeStruct((m, d), F32),
        compiler_params=_params(("arbitrary",)),
        name="out_proj",
    )(o, w, h)


def _ffn_kernel(hp_ref, h_ref, hn_ref, g_ref, wg_ref, wv_ref, cw_ref, cb_ref, wo_ref, out_ref,
                xn_ref, gate_ref, acc_ref, *, tm):
    i = pl.program_id(0)
    f = pl.program_id(1)

    @pl.when(f == 0)
    def _():
        gain = g_ref[...]
        prev = jnp.where(i > 0, _rms(hp_ref[...], gain), 0.0)
        nxt = jnp.where(i < pl.num_programs(0) - 1, _rms(hn_ref[...], gain), 0.0)
        xn_ref[0:HALO, :] = prev.astype(xn_ref.dtype)
        xn_ref[HALO:HALO + tm, :] = _rms(h_ref[...], gain).astype(xn_ref.dtype)
        xn_ref[HALO + tm:, :] = nxt.astype(xn_ref.dtype)
        acc_ref[...] = jnp.zeros_like(acc_ref)

    gate_ref[...] = _dot(xn_ref[...], wg_ref[...])
    val = _dot(xn_ref[HALO:HALO + tm, :], wv_ref[...])
    gc = cb_ref[...]
    for tap in range(CONV_W):
        lo = HALO - CONV_W // 2 + tap
        gc = gc + gate_ref[lo:lo + tm, :] * cw_ref[tap:tap + 1, :]
    a = gc * (1.0 / (1.0 + jnp.exp(-gc))) * val
    acc_ref[...] += _dot(a.astype(wo_ref.dtype), wo_ref[...])

    @pl.when(f == pl.num_programs(1) - 1)
    def _():
        out_ref[...] = h_ref[...] + acc_ref[...]


def _ffn(h, gain, w_in, conv_w, conv_b, w_out, layer):
    m, d = h.shape
    d_ff = w_out.shape[1]
    tm = _row_tile(m)
    tf = _col_tile(d_ff)
    n_i = m // tm
    hb = tm // HALO
    return pl.pallas_call(
        functools.partial(_ffn_kernel, tm=tm),
        grid=(n_i, d_ff // tf),
        in_specs=[
            pl.BlockSpec((HALO, d), lambda i, f: (jnp.maximum(i * hb - 1, 0), 0)),
            pl.BlockSpec((tm, d), lambda i, f: (i, 0)),
            pl.BlockSpec((HALO, d), lambda i, f: (jnp.minimum((i + 1) * hb, n_i * hb - 1), 0)),
            pl.BlockSpec((1, d), lambda i, f: (0, 0)),
            pl.BlockSpec((None, d, tf), lambda i, f: (layer, 0, f)),
            pl.BlockSpec((None, d, tf), lambda i, f: (layer, 0, d_ff // tf + f)),
            pl.BlockSpec((CONV_W, tf), lambda i, f: (0, f)),
            pl.BlockSpec((1, tf), lambda i, f: (0, f)),
            pl.BlockSpec((None, tf, d), lambda i, f: (layer, f, 0)),
        ],
        out_specs=pl.BlockSpec((tm, d), lambda i, f: (i, 0)),
        out_shape=jax.ShapeDtypeStruct((m, d), F32),
        scratch_shapes=[
            pltpu.VMEM((tm + 2 * HALO, d), COMPUTE_DTYPE),
            pltpu.VMEM((tm + 2 * HALO, tf), F32),
            pltpu.VMEM((tm, d), F32),
        ],
        compiler_params=_params(("arbitrary", "arbitrary")),
        name="conv_glu",
    )(h, h, h, gain.reshape(1, d), w_in, w_in, conv_w, conv_b.reshape(1, d_ff), w_out)


def _final_norm_kernel(h_ref, g_ref, o_ref):
    o_ref[...] = _rms(h_ref[...], g_ref[...])


def _final_norm(h, gain, batch, t):
    m, d = h.shape
    lseg = m // batch
    tt = _row_tile(t)
    return pl.pallas_call(
        _final_norm_kernel,
        grid=(batch, t // tt),
        in_specs=[
            pl.BlockSpec((pl.Element(tt), pl.Element(d)),
                         lambda b, i: (pl.multiple_of(b * lseg + META_BLOCK + i * tt, META_BLOCK), 0)),
            pl.BlockSpec((1, d), lambda b, i: (0, 0)),
        ],
        out_specs=pl.BlockSpec((None, tt, d), lambda b, i: (b, i, 0)),
        out_shape=jax.ShapeDtypeStruct((batch, t, d), F32),
        compiler_params=_params(("arbitrary", "arbitrary")),
        name="final_norm",
    )(h, gain.reshape(1, d))


def _rope_tables(batch, t):
    tok = jnp.arange(t)
    row = (tok // GRID_W).astype(F32)
    col = (tok % GRID_W).astype(F32)
    n_axis_pairs = HEAD_DIM // 4
    inv_freq = ROPE_THETA ** (-jnp.arange(n_axis_pairs, dtype=F32) / n_axis_pairs)
    ang = jnp.concatenate([row[:, None] * inv_freq[None], col[:, None] * inv_freq[None]], axis=-1)
    ang = jnp.concatenate([jnp.zeros((META_BLOCK, HEAD_DIM // 2), F32), ang], axis=0)
    cos, sin = jnp.cos(ang), jnp.sin(ang)
    cos = jnp.concatenate([cos, cos], axis=-1)
    sin = jnp.concatenate([-sin, sin], axis=-1)
    return jnp.tile(cos, (batch, 1)), jnp.tile(sin, (batch, 1))


def _trunk(x, meta_tokens, w):
    batch, t, d = x.shape
    assert t % GRID_W == 0 and d % HEAD_DIM == 0
    rows = t // GRID_W
    n_heads = d // HEAD_DIM
    scale = HEAD_DIM ** -0.5
    meta = jnp.broadcast_to(meta_tokens.astype(x.dtype)[None], (batch, N_META, d))
    h = jnp.concatenate([jnp.zeros((batch, META_OFF, d), x.dtype), meta, x], axis=1)
    h = h.reshape(batch * (META_BLOCK + t), d)
    depth = w["ffn_norm"].shape[0]
    rope = None
    for i in range(depth):
        j = i // 2
        if i % 2 == 0:
            qkv = _norm_proj(h, w["a_norm"][j], w["a_w_qkv"][j], n_heads, n_heads, scale)
            o = _na_attention(qkv, _na_bias_table(w["a_rpb"][j]), batch, rows, n_heads)
            h = _out_proj(o, w["a_w_o"][j], h)
        else:
            n_kv = (w["b_w_qkv"].shape[-1] // HEAD_DIM - n_heads) // 2
            if rope is None:
                rope = _rope_tables(batch, t)
            qkv = _norm_proj(h, w["b_norm"][j], w["b_w_qkv"][j], n_heads, n_kv, scale * LOG2E,
                             qk=(w["b_q_norm"][j], w["b_k_norm"][j]) + rope)
            o = _gqa_attention(qkv, batch, n_heads, n_kv)
            h = _out_proj(o, w["b_w_o"][j], h)
        h = _ffn(h, w["ffn_norm"][i], w["ffn_w_in"], w["ffn_conv_w"][i], w["ffn_conv_b"][i],
                 w["ffn_w_out"], i)
    return _final_norm(h, w["final_norm"], batch, t)


def kernel(x_prompt, x_sample, meta_tokens, a_norm, a_w_qkv, a_rpb, a_w_o, b_norm, b_w_qkv, b_q_norm,
           b_k_norm, b_w_o, ffn_norm, ffn_w_in, ffn_conv_w, ffn_conv_b, ffn_w_out, final_norm):
    cast = lambda a: a.astype(COMPUTE_DTYPE)
    w = dict(a_norm=a_norm, a_w_qkv=cast(a_w_qkv), a_rpb=a_rpb, a_w_o=cast(a_w_o), b_norm=b_norm,
             b_w_qkv=cast(b_w_qkv), b_q_norm=b_q_norm, b_k_norm=b_k_norm, b_w_o=cast(b_w_o),
             ffn_norm=ffn_norm, ffn_w_in=cast(ffn_w_in), ffn_conv_w=ffn_conv_w, ffn_conv_b=ffn_conv_b,
             ffn_w_out=cast(ffn_w_out), final_norm=final_norm)
    return (_trunk(x_prompt, meta_tokens, w), _trunk(x_sample, meta_tokens, w))
```
